```python
import math
import jax, jax.numpy as jnp
from jax import lax
import numpy as np

D_MODEL = 1024
BATCH = 8
SEQ = 4096
DEPTH = 4
DEC_BATCH = 32
DEC_SEQ = 2048
PAST_LEN = 128

N_META = 16
RMS_EPS = 1e-6
NEG = -1e30
M_HEADS = 4
M_DK = 128
M_DV = 256
M_CHUNK = 64
M_PAD = M_CHUNK - N_META
M_QK_W = M_HEADS * M_DK
M_V_W = M_HEADS * M_DV
M_IN_W = 2 * M_QK_W + 2 * M_V_W + 4 * M_HEADS
D_HEADS = 8
D_HD = 64
D_QK_W = D_HEADS * 2 * D_HD
D_V_W = D_HEADS * 2 * D_HD
D_IN_W = 2 * D_QK_W + D_V_W
ROPE_THETA = 500000.0
ROPE_DIM = D_HD // 4
Q_BLOCK = 128
FF = -(-8 * D_MODEL // (3 * 256)) * 256
N_M_LAYERS = (DEPTH + 1) // 2
N_D_LAYERS = DEPTH // 2

kernel_name = "hybrid_bidir_mlstm_diffattn_encoder"


def _rms_f32(x, g):
    xf = x.astype(jnp.float32)
    y = xf * lax.rsqrt(jnp.mean(xf * xf, axis=-1, keepdims=True) + RMS_EPS)
    return y * g.astype(jnp.float32)


def rmsnorm(x, g):
    return _rms_f32(x, g).astype(x.dtype)


def mlstm_cell(q, k, v, ig, lf):
    B, H, Lp, dk = q.shape
    dv = v.shape[-1]
    nc = Lp // M_CHUNK

    def chunks(t):
        return jnp.moveaxis(t.reshape((B, H, nc, M_CHUNK) + t.shape[3:]), 2, 0)

    tril = jnp.tril(jnp.ones((M_CHUNK, M_CHUNK), dtype=bool))

    def step(carry, xs):
        C, n, m = carry
        qc, kc, vc, igc, lfc = xs
        b = jnp.cumsum(lfc, axis=-1)
        dmat = jnp.where(tril, b[..., :, None] - b[..., None, :] + igc[..., None, :], -jnp.inf)
        m_inter = b + m[..., None]
        m_comb = jnp.maximum(jnp.max(dmat, axis=-1), m_inter)
        w_ts = jnp.exp(dmat - m_comb[..., None]) * jnp.einsum("bhtk,bhsk->bhts", qc, kc)
        scale = jnp.exp(m_inter - m_comb)
        num = jnp.einsum("bhts,bhsv->bhtv", w_ts, vc) + scale[..., None] * jnp.einsum("bhtk,bhvk->bhtv", qc, C)
        den = jnp.sum(w_ts, axis=-1) + scale * jnp.einsum("bhtk,bhk->bht", qc, n)
        h = num / jnp.maximum(jnp.abs(den), jnp.exp(-m_comb))[..., None]
        b_last = b[..., -1]
        a = b_last[..., None] - b + igc
        m_new = jnp.maximum(b_last + m, jnp.max(a, axis=-1))
        decay = jnp.exp(b_last + m - m_new)
        wk = jnp.exp(a - m_new[..., None])
        C_new = decay[..., None, None] * C + jnp.einsum("bhsv,bhsk->bhvk", vc * wk[..., None], kc)
        n_new = decay[..., None] * n + jnp.einsum("bhs,bhsk->bhk", wk, kc)
        return (C_new, n_new, m_new), h

    init = (jnp.zeros((B, H, dv, dk), jnp.float32),
            jnp.zeros((B, H, dk), jnp.float32),
            jnp.zeros((B, H), jnp.float32))
    _, h = lax.scan(step, init, (chunks(q), chunks(k), chunks(v), chunks(ig), chunks(lf)))
    return jnp.moveaxis(h, 0, 2).reshape(B, H, Lp, dv)


def mlstm_mixer(xn, w_in, b_gate, g_head, w_out):
    B, L, _ = xn.shape
    proj = xn @ w_in
    q = proj[..., :M_QK_W]
    k = proj[..., M_QK_W:2 * M_QK_W]
    v = proj[..., 2 * M_QK_W:2 * M_QK_W + M_V_W]
    o = proj[..., 2 * M_QK_W + M_V_W:2 * M_QK_W + 2 * M_V_W]
    gpre = proj[..., 2 * M_QK_W + 2 * M_V_W:]

    def heads(t, d):
        t = t.astype(jnp.float32).reshape(B, L, M_HEADS, d)
        t = jnp.pad(t, ((0, 0), (M_PAD, 0), (0, 0), (0, 0)))
        return t.transpose(0, 2, 1, 3)

    qh = heads(q, M_DK) * (M_DK ** -0.5)
    kh = heads(k, M_DK)
    vh = heads(v, M_DV)
    gates = (gpre.astype(jnp.float32) + b_gate.astype(jnp.float32)).reshape(B, L, 4, M_HEADS)
    gates = jnp.pad(gates, ((0, 0), (M_PAD, 0), (0, 0), (0, 0))).transpose(2, 0, 3, 1)
    valid = jnp.arange(L + M_PAD) >= M_PAD
    ig = jnp.where(valid, gates[0::2], NEG)
    lf = jnp.where(valid, jax.nn.log_sigmoid(gates[1::2]), 0.0)

    def flip(t):
        return jnp.flip(t, axis=2)

    h_f = mlstm_cell(qh, kh, vh, ig[0], lf[0])
    h_b = flip(mlstm_cell(flip(qh), flip(kh), flip(vh), flip(ig[1]), flip(lf[1])))
    h = (h_f + h_b)[:, :, M_PAD:]
    h = _rms_f32(h, g_head)
    h = h.transpose(0, 2, 1, 3).reshape(B, L, M_V_W)
    h = (h * jax.nn.sigmoid(o.astype(jnp.float32))).astype(xn.dtype)
    return h @ w_out


def rope_tables(L):
    pos = jnp.arange(L, dtype=jnp.float32)
    inv = ROPE_THETA ** (-jnp.arange(0, ROPE_DIM, 2, dtype=jnp.float32) / ROPE_DIM)
    ang = pos[:, None] * inv[None, :]
    return jnp.cos(ang), jnp.sin(ang)


def partial_rope(x, cos, sin):
    half = ROPE_DIM // 2
    x1 = x[..., :half]
    x2 = x[..., half:ROPE_DIM]
    rest = x[..., ROPE_DIM:]
    c = cos[None, :, None, None, :]
    s = sin[None, :, None, None, :]
    return jnp.concatenate([x1 * c - x2 * s, x2 * c + x1 * s, rest], axis=-1)


def diff_attn(xn, w_in, g_q, g_k, lam_q1, lam_k1, lam_q2, lam_k2, g_sub, w_out, lambda_init):
    B, L, _ = xn.shape
    proj = xn @ w_in
    q = proj[..., :D_QK_W].reshape(B, L, D_HEADS, 2, D_HD)
    k = proj[..., D_QK_W:2 * D_QK_W].reshape(B, L, D_HEADS, 2, D_HD)
    v = proj[..., 2 * D_QK_W:].reshape(B, L, D_HEADS, 2 * D_HD).astype(jnp.float32)
    cos, sin = rope_tables(L)
    q = partial_rope(_rms_f32(q, g_q), cos, sin) * (D_HD ** -0.5)
    k = partial_rope(_rms_f32(k, g_k), cos, sin)
    q = q.transpose(0, 2, 3, 1, 4)
    k = k.transpose(0, 2, 3, 1, 4)
    v = v.transpose(0, 2, 1, 3)
    lam = (jnp.exp(jnp.sum(lam_q1.astype(jnp.float32) * lam_k1.astype(jnp.float32)))
           - jnp.exp(jnp.sum(lam_q2.astype(jnp.float32) * lam_k2.astype(jnp.float32)))
           + lambda_init)
    nb = -(-L // Q_BLOCK)
    Lq = nb * Q_BLOCK
    qb = jnp.pad(q, ((0, 0), (0, 0), (0, 0), (0, Lq - L), (0, 0))).reshape(B, D_HEADS, 2, nb, Q_BLOCK, D_HD)
    qb = jnp.moveaxis(qb, 3, 0)

    def block(qblk):
        s = jnp.einsum("bhcqd,bhckd->bhcqk", qblk, k)
        p = jax.nn.softmax(s, axis=-1)
        a = p[:, :, 0] - lam * p[:, :, 1]
        return jnp.einsum("bhqk,bhkv->bhqv", a, v)

    o = lax.map(block, qb)
    o = jnp.moveaxis(o, 0, 2).reshape(B, D_HEADS, Lq, 2 * D_HD)[:, :, :L]
    o = _rms_f32(o, g_sub) * (1.0 - lambda_init)
    o = o.transpose(0, 2, 1, 3).reshape(B, L, D_V_W).astype(xn.dtype)
    return o @ w_out


def swiglu(xn, w_in, w_out):
    gu = xn @ w_in
    g = gu[..., :FF]
    u = gu[..., FF:]
    return (jax.nn.silu(g) * u) @ w_out


def trunk(x, meta_tokens, norm_mix, norm_ffn, m_w_in, m_b_gate, m_g_head, m_w_out,
          d_w_in, d_g_q, d_g_k, d_lam_q1, d_lam_k1, d_lam_q2, d_lam_k2, d_g_sub, d_w_out,
          ffn_w_in, ffn_w_out):
    B = x.shape[0]
    meta = jnp.broadcast_to(meta_tokens[None].astype(x.dtype), (B, N_META, D_MODEL))
    h = jnp.concatenate([meta, x], axis=1)
    for i in range(DEPTH):
        j = i // 2
        xn = rmsnorm(h, norm_mix[i])
        if i % 2 == 0:
            h = h + mlstm_mixer(xn, m_w_in[j], m_b_gate[j], m_g_head[j], m_w_out[j])
        else:
            lambda_init = 0.8 - 0.6 * math.exp(-0.3 * i)
            h = h + diff_attn(xn, d_w_in[j], d_g_q[j], d_g_k[j], d_lam_q1[j], d_lam_k1[j],
                              d_lam_q2[j], d_lam_k2[j], d_g_sub[j], d_w_out[j], lambda_init)
        xn = rmsnorm(h, norm_ffn[i])
        h = h + swiglu(xn, ffn_w_in[i], ffn_w_out[i])
    return h[:, N_META:]


def setup_inputs(seed: int = 0) -> dict:
    key = jax.random.key(seed)
    ks = jax.random.split(key, 24)
    f32 = jnp.float32

    def nrm(k, shape, scale):
        return jax.random.normal(k, shape, f32) * scale

    fg_off = jnp.concatenate([jnp.zeros((M_HEADS,), f32), jnp.linspace(3.0, 6.0, M_HEADS, dtype=f32),
                              jnp.zeros((M_HEADS,), f32), jnp.linspace(3.0, 6.0, M_HEADS, dtype=f32)])
    return {
        "x_prompt": nrm(ks[0], (BATCH, SEQ, D_MODEL), 1.0),
        "x_sample": nrm(ks[1], (DEC_BATCH, DEC_SEQ, D_MODEL), 1.0),
        "meta_tokens": nrm(ks[2], (N_META, D_MODEL), 1.0),
        "norm_mix": 1.0 + nrm(ks[3], (DEPTH, D_MODEL), 0.05),
        "norm_ffn": 1.0 + nrm(ks[4], (DEPTH, D_MODEL), 0.05),
        "m_w_in": nrm(ks[5], (N_M_LAYERS, D_MODEL, M_IN_W), D_MODEL ** -0.5),
        "m_b_gate": fg_off[None, :] + nrm(ks[6], (N_M_LAYERS, 4 * M_HEADS), 0.1),
        "m_g_head": 1.0 + nrm(ks[7], (N_M_LAYERS, M_DV), 0.05),
        "m_w_out": nrm(ks[8], (N_M_LAYERS, M_V_W, D_MODEL), M_V_W ** -0.5),
        "d_w_in": nrm(ks[9], (N_D_LAYERS, D_MODEL, D_IN_W), D_MODEL ** -0.5),
        "d_g_q": 1.0 + nrm(ks[10], (N_D_LAYERS, D_HD), 0.05),
        "d_g_k": 1.0 + nrm(ks[11], (N_D_LAYERS, D_HD), 0.05),
        "d_lam_q1": nrm(ks[12], (N_D_LAYERS, D_HD), 0.1),
        "d_lam_k1": nrm(ks[13], (N_D_LAYERS, D_HD), 0.1),
        "d_lam_q2": nrm(ks[14], (N_D_LAYERS, D_HD), 0.1),
        "d_lam_k2": nrm(ks[15], (N_D_LAYERS, D_HD), 0.1),
        "d_g_sub": 1.0 + nrm(ks[16], (N_D_LAYERS, 2 * D_HD), 0.05),
        "d_w_out": nrm(ks[17], (N_D_LAYERS, D_V_W, D_MODEL), D_V_W ** -0.5),
        "ffn_w_in": nrm(ks[18], (DEPTH, D_MODEL, 2 * FF), D_MODEL ** -0.5),
        "ffn_w_out": nrm(ks[19], (DEPTH, FF, D_MODEL), FF ** -0.5),
    }


def reference(x_prompt, x_sample, meta_tokens, norm_mix, norm_ffn, m_w_in, m_b_gate, m_g_head, m_w_out,
              d_w_in, d_g_q, d_g_k, d_lam_q1, d_lam_k1, d_lam_q2, d_lam_k2, d_g_sub, d_w_out,
              ffn_w_in, ffn_w_out):
    y_prompt = trunk(x_prompt, meta_tokens, norm_mix, norm_ffn, m_w_in, m_b_gate, m_g_head, m_w_out,
                     d_w_in, d_g_q, d_g_k, d_lam_q1, d_lam_k1, d_lam_q2, d_lam_k2, d_g_sub, d_w_out,
                     ffn_w_in, ffn_w_out)
    y_sample = trunk(x_sample, meta_tokens, norm_mix, norm_ffn, m_w_in, m_b_gate, m_g_head, m_w_out,
                     d_w_in, d_g_q, d_g_k, d_lam_q1, d_lam_k1, d_lam_q2, d_lam_k2, d_g_sub, d_w_out,
                     ffn_w_in, ffn_w_out)
    return (y_prompt, y_sample)
```

```python
import functools
import math

import jax
import jax.numpy as jnp
from jax import lax
from jax.experimental import pallas as pl
from jax.experimental.pallas import tpu as pltpu

F32 = jnp.float32
BF16 = jnp.bfloat16

D_MODEL = 1024
DEPTH = 4
N_META = 16
RMS_EPS = 1e-6
NEG = -1e30
M_HEADS = 4
M_DK = 128
M_DV = 256
M_CHUNK = 64
M_PAD = M_CHUNK - N_META
M_QK_W = M_HEADS * M_DK
M_V_W = M_HEADS * M_DV
M_MAIN_W = 2 * M_QK_W + 2 * M_V_W
D_HEADS = 8
D_HD = 64
D_QK_W = D_HEADS * 2 * D_HD
D_V_W = D_HEADS * 2 * D_HD
ROPE_THETA = 500000.0
ROPE_DIM = D_HD // 4
FF = 2816
TAIL = M_CHUNK
LANES = 128
VMEM_LIMIT = 56 * 1024 * 1024


def _params(*sem):
    return pltpu.CompilerParams(dimension_semantics=sem, vmem_limit_bytes=VMEM_LIMIT)


def _row_tile(rows, target):
    best = 16
    for t in range(16, target + 1, 16):
        if rows % t == 0:
            best = t
    return best


def _rms_rows(x, g):
    ms = jnp.mean(x * x, axis=-1, keepdims=True)
    return x * lax.rsqrt(ms + RMS_EPS) * g


def _norm_mm_kernel(x_ref, g_ref, w_ref, b_ref, o_ref, xn_ref):
    @pl.when(pl.program_id(1) == 0)
    def _():
        xn_ref[...] = _rms_rows(x_ref[...], g_ref[...]).astype(BF16)

    acc = jnp.dot(xn_ref[...], w_ref[...], preferred_element_type=F32)
    o_ref[...] = (acc + b_ref[...]).astype(o_ref.dtype)


def _norm_mm(x, g, w, b, out_dtype, tn, name):
    rows, d = x.shape
    n = w.shape[1]
    tm = _row_tile(rows, 1024)
    return pl.pallas_call(
        _norm_mm_kernel,
        grid=(rows // tm, n // tn),
        in_specs=[
            pl.BlockSpec((tm, d), lambda i, j: (i, 0)),
            pl.BlockSpec((1, d), lambda i, j: (0, 0)),
            pl.BlockSpec((d, tn), lambda i, j: (0, j)),
            pl.BlockSpec((1, tn), lambda i, j: (0, j)),
        ],
        out_specs=pl.BlockSpec((tm, tn), lambda i, j: (i, j)),
        out_shape=jax.ShapeDtypeStruct((rows, n), out_dtype),
        scratch_shapes=[pltpu.VMEM((tm, d), BF16)],
        compiler_params=_params("arbitrary", "arbitrary"),
        name=name,
    )(x, g, w, b)


def _mm_res_kernel(a_ref, w_ref, r_ref, o_ref):
    o_ref[...] = r_ref[...] + jnp.dot(a_ref[...], w_ref[...], preferred_element_type=F32)


def _mm_res(a, w, r, name):
    rows, k = a.shape
    n = w.shape[1]
    tm = _row_tile(rows, 1024)
    return pl.pallas_call(
        _mm_res_kernel,
        grid=(rows // tm,),
        in_specs=[
            pl.BlockSpec((tm, k), lambda i: (i, 0)),
            pl.BlockSpec((k, n), lambda i: (0, 0)),
            pl.BlockSpec((tm, n), lambda i: (i, 0)),
        ],
        out_specs=pl.BlockSpec((tm, n), lambda i: (i, 0)),
        out_shape=jax.ShapeDtypeStruct((rows, n), F32),
        compiler_params=_params("arbitrary"),
        name=name,
    )(a, w, r)


FF_CHUNK = 512


def _ffn_kernel(x_ref, g_ref, wi_ref, wo_ref, o_ref, xn_ref, acc_ref):
    x = x_ref[...]
    xn_ref[...] = _rms_rows(x, g_ref[...]).astype(BF16)
    acc_ref[...] = x
    for c0 in range(0, FF, FF_CHUNK):
        c1 = min(c0 + FF_CHUNK, FF)
        xn = xn_ref[...]
        gate = jnp.dot(xn, wi_ref[:, c0:c1], preferred_element_type=F32)
        up = jnp.dot(xn, wi_ref[:, FF + c0:FF + c1], preferred_element_type=F32)
        act = (gate * (1.0 / (1.0 + jnp.exp(-gate))) * up).astype(BF16)
        acc_ref[...] += jnp.dot(act, wo_ref[c0:c1, :], preferred_element_type=F32)
    o_ref[...] = acc_ref[...]


def _ffn(x, g, wi, wo, name):
    rows, d = x.shape
    tm = _row_tile(rows, 512)
    return pl.pallas_call(
        _ffn_kernel,
        grid=(rows // tm,),
        in_specs=[
            pl.BlockSpec((tm, d), lambda i: (i, 0)),
            pl.BlockSpec((1, d), lambda i: (0, 0)),
            pl.BlockSpec((d, 2 * FF), lambda i: (0, 0), pipeline_mode=pl.Buffered(1)),
            pl.BlockSpec((FF, d), lambda i: (0, 0), pipeline_mode=pl.Buffered(1)),
        ],
        out_specs=pl.BlockSpec((tm, d), lambda i: (i, 0)),
        out_shape=jax.ShapeDtypeStruct((rows, d), F32),
        scratch_shapes=[pltpu.VMEM((tm, d), BF16), pltpu.VMEM((tm, d), F32)],
        compiler_params=_params("arbitrary"),
        name=name,
    )(x, g, wi, wo)


def _split3(x):
    hi = x.astype(BF16)
    r1 = x - hi.astype(F32)
    mid = r1.astype(BF16)
    lo = (r1 - mid.astype(F32)).astype(BF16)
    return hi, mid, lo


def _dot_exact01(m01, x, dims):
    out = None
    for part in _split3(x):
        t = lax.dot_general(m01, part, dims, preferred_element_type=F32)
        out = t if out is None else out + t
    return out


_NN = (((1,), (0,)), ((), ()))
_NT = (((1,), (1,)), ((), ()))
_TN = (((0,), (0,)), ((), ()))


def _gate_prep_kernel(g_ref, row_ref, col_ref, *, seq, rows_per_block):
    rb = rows_per_block
    g = g_ref[0]
    row0 = pl.program_id(1) * rb
    r = row0 + lax.broadcasted_iota(jnp.int32, (rb, LANES), 0)
    valid = jnp.logical_or(r < seq, r >= seq + M_PAD)
    logf = jnp.where(valid, jnp.minimum(g, 0.0) - jnp.log(1.0 + jnp.exp(-jnp.abs(g))), 0.0)
    ig = jnp.where(valid, g, NEG)
    ti = lax.broadcasted_iota(jnp.int32, (rb, rb), 0)
    si = lax.broadcasted_iota(jnp.int32, (rb, rb), 1)
    same = (ti // M_CHUNK) == (si // M_CHUNK)
    pre_m = jnp.where(jnp.logical_and(same, si <= ti), 1.0, 0.0).astype(BF16)
    suf_m = jnp.where(jnp.logical_and(same, si >= ti), 1.0, 0.0).astype(BF16)
    pre = _dot_exact01(pre_m, logf, _NN)
    suf = _dot_exact01(suf_m, logf, _NN)
    lane = lax.broadcasted_iota(jnp.int32, (rb, LANES), 1)
    pre_dn = pltpu.roll(pre, LANES - M_HEADS, 1)
    suf_dn = pltpu.roll(suf, LANES - M_HEADS, 1)
    row = jnp.where(lane < 4, ig - pre_dn,
                    jnp.where(lane < 8, pre,
                              jnp.where(lane < 12, ig - suf_dn, suf)))
    row = jnp.where(lane < 16, row, 0.0)
    row_ref[0] = row
    er = lax.broadcasted_iota(jnp.int32, (8, LANES), 0)
    el = lax.broadcasted_iota(jnp.int32, (8, LANES), 1)
    pick = jnp.where(el == jnp.where(er < 4, er, er + 4), 1.0, 0.0).astype(BF16)
    col = _dot_exact01(pick, row, _NT)
    for c in range(rb // M_CHUNK):
        col_ref[0, c] = col[:, c * M_CHUNK:(c + 1) * M_CHUNK]


def _gate_prep(g, seq, name):
    b, lp, _ = g.shape
    nc = lp // M_CHUNK
    cpb = max(c for c in (1, 2, 3, 4, 5, 6) if nc % c == 0)
    rb = cpb * M_CHUNK
    return pl.pallas_call(
        functools.partial(_gate_prep_kernel, seq=seq, rows_per_block=rb),
        grid=(b, nc // cpb),
        in_specs=[pl.BlockSpec((1, rb, LANES), lambda i, j: (i, j, 0))],
        out_specs=[
            pl.BlockSpec((1, rb, LANES), lambda i, j: (i, j, 0)),
            pl.BlockSpec((1, cpb, 8, M_CHUNK), lambda i, j: (i, j, 0, 0)),
        ],
        out_shape=[
            jax.ShapeDtypeStruct((b, lp, LANES), F32),
            jax.ShapeDtypeStruct((b, nc, 8, M_CHUNK), F32),
        ],
        compiler_params=_params("arbitrary", "arbitrary"),
        name=name,
    )(g)


def _lane_pick(x, idx):
    lane = lax.broadcasted_iota(jnp.int32, x.shape, 1)
    return jnp.sum(jnp.where(lane == idx, x, 0.0), axis=1, keepdims=True)


def _mlstm_kernel(q_ref, k_ref, v_ref, o_ref, gr_ref, gc_ref, gh_ref, out_ref,
                  hf_ref, hb_ref, ct_ref, n_ref, m_ref, *, seq):
    head = pl.program_id(1)
    nx = seq // M_CHUNK
    nc = nx + 1
    qk_scale = M_DK ** -0.5
    ti = lax.broadcasted_iota(jnp.int32, (M_CHUNK, M_CHUNK), 0)
    si = lax.broadcasted_iota(jnp.int32, (M_CHUNK, M_CHUNK), 1)
    ct_ref[...] = jnp.zeros_like(ct_ref)
    n_ref[...] = jnp.zeros_like(n_ref)
    m_ref[...] = jnp.zeros_like(m_ref)

    def step(d, c):
        rows = pl.ds(pl.multiple_of(c * M_CHUNK, M_CHUNK), M_CHUNK)
        q = q_ref[0, rows, :]
        k = k_ref[0, rows, :]
        v = v_ref[0, rows, :]
        grow = gr_ref[0, rows, :]
        alpha_r = _lane_pick(grow, 8 * d + head)
        beta_r = _lane_pick(grow, 8 * d + 4 + head)
        alpha_c = gc_ref[0, c, pl.ds(4 * d + head, 1), :]
        m_prev = m_ref[d]
        ct = ct_ref[d]
        nvec = n_ref[d]
        keep = (si <= ti) if d == 0 else (si >= ti)
        dmat = jnp.where(keep, beta_r + alpha_c, -jnp.inf)
        m_inter = beta_r + m_prev
        m_comb = jnp.maximum(jnp.max(dmat, axis=1, keepdims=True), m_inter)
        s = lax.dot_general(q, k, _NT, preferred_element_type=F32) * qk_scale
        w = jnp.exp(dmat - m_comb) * s
        scale = jnp.exp(m_inter - m_comb)
        q_c = jnp.dot(q, ct.astype(BF16), preferred_element_type=F32) * qk_scale
        num = jnp.dot(w.astype(BF16), v, preferred_element_type=F32) + scale * q_c
        q_n = jnp.sum(q.astype(F32) * nvec, axis=1, keepdims=True) * qk_scale
        den = jnp.sum(w, axis=1, keepdims=True) + scale * q_n
        hval = num / jnp.maximum(jnp.abs(den), jnp.exp(-m_comb))
        if d == 0:
            hf_ref[rows, :] = hval
            tot = beta_r[M_CHUNK - 1:M_CHUNK]
        else:
            hb_ref[rows, :] = hval
            tot = beta_r[0:1]
        a_max = jnp.max(alpha_c, axis=1, keepdims=True)
        m_new = jnp.maximum(tot + m_prev, tot + a_max)
        decay = jnp.exp(tot + m_prev - m_new)
        wk = jnp.exp(alpha_r + tot - m_new)
        vw = (v.astype(F32) * wk).astype(BF16)
        ct_ref[d] = decay * ct + lax.dot_general(k, vw, _TN, preferred_element_type=F32)
        n_ref[d] = decay * nvec + jnp.sum(k.astype(F32) * wk, axis=0, keepdims=True)
        m_ref[d] = m_new

    def body(i, carry):
        cf = jnp.where(i == 0, nx, i - 1)
        cb = jnp.where(i == nx, nx, nx - 1 - i)
        step(0, cf)
        step(1, cb)
        return carry

    lax.fori_loop(0, nc, body, 0)

    def finish(c, carry):
        rows = pl.ds(pl.multiple_of(c * M_CHUNK, M_CHUNK), M_CHUNK)
        hsum = hf_ref[rows, :] + hb_ref[rows, :]
        og = o_ref[0, rows, :].astype(F32)
        out_ref[0, rows, :] = (_rms_rows(hsum, gh_ref[...]) * (1.0 / (1.0 + jnp.exp(-og)))).astype(BF16)
        return carry

    lax.fori_loop(0, nc, finish, 0)


def _mlstm(proj, grow, gcol, g_head, seq, name):
    b, lp, _ = proj.shape
    nc = lp // M_CHUNK
    return pl.pallas_call(
        functools.partial(_mlstm_kernel, seq=seq),
        grid=(b, M_HEADS),
        in_specs=[
            pl.BlockSpec((1, lp, M_DK), lambda i, h: (i, 0, h)),
            pl.BlockSpec((1, lp, M_DK), lambda i, h: (i, 0, M_HEADS + h)),
            pl.BlockSpec((1, lp, M_DV), lambda i, h: (i, 0, M_HEADS + h)),
            pl.BlockSpec((1, lp, M_DV), lambda i, h: (i, 0, 2 * M_HEADS + h)),
            pl.BlockSpec((1, lp, LANES), lambda i, h: (i, 0, 0)),
            pl.BlockSpec((1, nc, 8, M_CHUNK), lambda i, h: (i, 0, 0, 0)),
            pl.BlockSpec((1, M_DV), lambda i, h: (0, 0)),
        ],
        out_specs=pl.BlockSpec((1, lp, M_DV), lambda i, h: (i, 0, h)),
        out_shape=jax.ShapeDtypeStruct((b, lp, M_V_W), BF16),
        scratch_shapes=[
            pltpu.VMEM((lp, M_DV), F32),
            pltpu.VMEM((lp, M_DV), F32),
            pltpu.VMEM((2, M_DK, M_DV), F32),
            pltpu.VMEM((2, 1, M_DK), F32),
            pltpu.VMEM((2, 1, 1), F32),
        ],
        compiler_params=_params("arbitrary", "arbitrary"),
        name=name,
    )(proj, proj, proj, proj, grow, gcol, g_head)


ATTN_TK = 1024


def _dot_exact01_rhs(x, m01):
    out = None
    for part in _split3(x):
        t = jnp.dot(part, m01, preferred_element_type=F32)
        out = t if out is None else out + t
    return out


def _norm_rope_rows(x, gain, cos_t, sin_dn, sin_up):
    xf = x.astype(F32)
    gi = lax.broadcasted_iota(jnp.int32, (LANES, LANES), 0) // D_HD
    gj = lax.broadcasted_iota(jnp.int32, (LANES, LANES), 1) // D_HD
    group = jnp.where(gi == gj, 1.0, 0.0).astype(BF16)
    ssq = _dot_exact01_rhs(xf * xf, group)
    y = xf * lax.rsqrt(ssq * (1.0 / D_HD) + RMS_EPS) * gain
    half = ROPE_DIM // 2
    return y * cos_t + pltpu.roll(y, LANES - half, 1) * sin_dn + pltpu.roll(y, half, 1) * sin_up


def _attn_kernel(q_ref, k_ref, v_ref, cos_ref, sdn_ref, sup_ref, gq_ref, gk_ref, lam_ref, gs_ref,
                 out_ref, kn_ref, *, seq, tq, lambda_init):
    qi = pl.program_id(2)
    lp = seq + TAIL
    tk = min(ATTN_TK, seq)

    @pl.when(qi == 0)
    def _():
        def prep(c, carry):
            rows = pl.ds(pl.multiple_of(c * tq, tq), tq)
            kn_ref[rows, :] = _norm_rope_rows(k_ref[0, rows, :], gk_ref[...], cos_ref[rows, :],
                                              sdn_ref[rows, :], sup_ref[rows, :]).astype(BF16)
            return carry
        lax.fori_loop(0, lp // tq, prep, 0)

    qrows = pl.ds(pl.multiple_of(qi * tq, tq), tq)
    qn = _norm_rope_rows(q_ref[0], gq_ref[...], cos_ref[qrows, :], sdn_ref[qrows, :],
                         sup_ref[qrows, :]) * (D_HD ** -0.5)
    lane = lax.broadcasted_iota(jnp.int32, (tq, LANES), 1)
    qs = (jnp.where(lane < D_HD, qn, 0.0).astype(BF16), jnp.where(lane >= D_HD, qn, 0.0).astype(BF16))

    kt = kn_ref[seq:lp, :]
    vt = v_ref[0, seq:lp, :]
    tail_ok = lax.broadcasted_iota(jnp.int32, (tq, TAIL), 1) >= M_PAD
    state = []
    for qc in qs:
        s = jnp.where(tail_ok, lax.dot_general(qc, kt, _NT, preferred_element_type=F32), NEG)
        m = jnp.max(s, axis=1, keepdims=True)
        e = jnp.exp(s - m)
        state += [m, jnp.sum(e, axis=1, keepdims=True),
                  jnp.dot(e.astype(BF16), vt, preferred_element_type=F32)]

    def body(j, st):
        rows = pl.ds(pl.multiple_of(j * tk, tk), tk)
        kc = kn_ref[rows, :]
        vc = v_ref[0, rows, :]
        new = []
        for c, qc in enumerate(qs):
            m, l, acc = st[3 * c:3 * c + 3]
            s = lax.dot_general(qc, kc, _NT, preferred_element_type=F32)
            m_new = jnp.maximum(m, jnp.max(s, axis=1, keepdims=True))
            a = jnp.exp(m - m_new)
            e = jnp.exp(s - m_new)
            new += [m_new, a * l + jnp.sum(e, axis=1, keepdims=True),
                    a * acc + jnp.dot(e.astype(BF16), vc, preferred_element_type=F32)]
        return tuple(new)

    m1, l1, acc1, m2, l2, acc2 = lax.fori_loop(0, seq // tk, body, tuple(state))
    lam_p = lam_ref[...]
    lam = (jnp.exp(jnp.sum(lam_p[0:1] * lam_p[1:2], axis=1, keepdims=True))
           - jnp.exp(jnp.sum(lam_p[2:3] * lam_p[3:4], axis=1, keepdims=True)) + lambda_init)
    o = acc1 / l1 - lam * (acc2 / l2)
    out_ref[0] = (_rms_rows(o, gs_ref[...]) * (1.0 - lambda_init)).astype(BF16)


def _attn(proj, tables, gq, gk, lam, gs, seq, lambda_init, name):
    b, lp, _ = proj.shape
    tq = _row_tile(lp, 512)
    cos_t, sin_dn, sin_up = tables
    full = lambda i, h, j: (0, 0)
    return pl.pallas_call(
        functools.partial(_attn_kernel, seq=seq, tq=tq, lambda_init=lambda_init),
        grid=(b, D_HEADS, lp // tq),
        in_specs=[
            pl.BlockSpec((1, tq, LANES), lambda i, h, j: (i, j, h)),
            pl.BlockSpec((1, lp, LANES), lambda i, h, j: (i, 0, D_HEADS + h)),
            pl.BlockSpec((1, lp, LANES), lambda i, h, j: (i, 0, 2 * D_HEADS + h)),
            pl.BlockSpec((lp, LANES), full),
            pl.BlockSpec((lp, LANES), full),
            pl.BlockSpec((lp, LANES), full),
            pl.BlockSpec((1, LANES), full),
            pl.BlockSpec((1, LANES), full),
            pl.BlockSpec((4, D_HD), full),
            pl.BlockSpec((1, LANES), full),
        ],
        out_specs=pl.BlockSpec((1, tq, LANES), lambda i, h, j: (i, j, h)),
        out_shape=jax.ShapeDtypeStruct((b, lp, D_V_W), BF16),
        scratch_shapes=[pltpu.VMEM((lp, LANES), BF16)],
        compiler_params=_params("arbitrary", "arbitrary", "arbitrary"),
        name=name,
    )(proj, proj, proj, cos_t, sin_dn, sin_up, gq, gk, lam, gs)


def _rope_tables(seq):
    lp = seq + TAIL
    r = jnp.arange(lp)
    pos = jnp.where(r < seq, r + N_META, jnp.maximum(r - seq - M_PAD, 0)).astype(F32)
    half = ROPE_DIM // 2
    inv = ROPE_THETA ** (-jnp.arange(0, ROPE_DIM, 2, dtype=F32) / ROPE_DIM)
    ang = pos[:, None] * inv[None, :]
    cos, sin = jnp.cos(ang), jnp.sin(ang)
    one = jnp.ones((lp, D_HD - ROPE_DIM), F32)
    zero8 = jnp.zeros((lp, half), F32)
    zero = jnp.zeros((lp, D_HD - ROPE_DIM), F32)
    cos_t = jnp.concatenate([cos, cos, one], axis=1)
    sin_dn = jnp.concatenate([-sin, zero8, zero], axis=1)
    sin_up = jnp.concatenate([zero8, sin, zero], axis=1)
    return tuple(jnp.tile(t, (1, 2)) for t in (cos_t, sin_dn, sin_up))


def _trunk(x, meta_tokens, norm_mix, norm_ffn, m_w_main, m_w_gate, m_b_gate, m_g_head, m_w_out,
           d_w_in, d_g_q, d_g_k, d_lam, d_g_sub, d_w_out, ffn_w_in, ffn_w_out, tag):
    b, seq, d = x.shape
    lp = seq + TAIL
    rows = b * lp
    meta = jnp.broadcast_to(meta_tokens[None].astype(x.dtype), (b, N_META, d))
    h = jnp.concatenate([x, jnp.zeros((b, M_PAD, d), x.dtype), meta], axis=1).reshape(rows, d)
    tables = _rope_tables(seq)
    zero_main = jnp.zeros((1, M_MAIN_W), F32)
    for i in range(DEPTH):
        j = i // 2
        gmix = norm_mix[i][None]
        if i % 2 == 0:
            proj = _norm_mm(h, gmix, m_w_main[j], zero_main, BF16, 1024, f"m_in_{tag}{i}")
            gates = _norm_mm(h, gmix, m_w_gate[j], m_b_gate[j], F32, LANES, f"m_gate_{tag}{i}")
            grow, gcol = _gate_prep(gates.reshape(b, lp, LANES), seq, f"m_prep_{tag}{i}")
            mixed = _mlstm(proj.reshape(b, lp, M_MAIN_W), grow, gcol, m_g_head[j][None], seq,
                           f"m_cell_{tag}{i}")
            h = _mm_res(mixed.reshape(rows, M_V_W), m_w_out[j], h, f"m_out_{tag}{i}")
        else:
            lambda_init = 0.8 - 0.6 * math.exp(-0.3 * i)
            proj = _norm_mm(h, gmix, d_w_in[j], zero_main, BF16, 1024, f"d_in_{tag}{i}")
            mixed = _attn(proj.reshape(b, lp, 3 * D_QK_W), tables, d_g_q[j], d_g_k[j], d_lam[j],
                          d_g_sub[j], seq, lambda_init, f"d_attn_{tag}{i}")
            h = _mm_res(mixed.reshape(rows, D_V_W), d_w_out[j], h, f"d_out_{tag}{i}")
        h = _ffn(h, norm_ffn[i][None], ffn_w_in[i], ffn_w_out[i], f"ffn_{tag}{i}")
    return h.reshape(b, lp, d)[:, :seq]


def kernel(x_prompt, x_sample, meta_tokens, norm_mix, norm_ffn, m_w_in, m_b_gate, m_g_head, m_w_out,
           d_w_in, d_g_q, d_g_k, d_lam_q1, d_lam_k1, d_lam_q2, d_lam_k2, d_g_sub, d_w_out,
           ffn_w_in, ffn_w_out):
    n_gate = 4 * M_HEADS
    m_w_main = m_w_in[:, :, :M_MAIN_W].astype(BF16)
    m_w_gate = jnp.pad(m_w_in[:, :, M_MAIN_W:], ((0, 0), (0, 0), (0, LANES - n_gate))).astype(BF16)
    b_gate = jnp.pad(m_b_gate.astype(F32), ((0, 0), (0, LANES - n_gate)))[:, None, :]
    tile2 = lambda t: jnp.tile(t.astype(F32), (1, 2))[:, None, :]
    d_lam = jnp.stack([d_lam_q1, d_lam_k1, d_lam_q2, d_lam_k2], axis=1).astype(F32)
    args = (meta_tokens, norm_mix.astype(F32), norm_ffn.astype(F32), m_w_main, m_w_gate, b_gate,
            m_g_head.astype(F32), m_w_out.astype(BF16), d_w_in.astype(BF16), tile2(d_g_q), tile2(d_g_k),
            d_lam, d_g_sub.astype(F32)[:, None, :], d_w_out.astype(BF16), ffn_w_in.astype(BF16),
            ffn_w_out.astype(BF16))
    return (_trunk(x_prompt, *args, "p"), _trunk(x_sample, *args, "s"))
```

```python
import functools
import math

import jax
import jax.numpy as jnp
from jax import lax
from jax.experimental import pallas as pl
from jax.experimental.pallas import tpu as pltpu

F32 = jnp.float32
BF16 = jnp.bfloat16

D_MODEL = 1024
DEPTH = 4
N_META = 16
RMS_EPS = 1e-6
NEG = -1e30
M_HEADS = 4
M_DK = 128
M_DV = 256
M_CHUNK = 64
M_XCHUNK = 256
M_PAD = M_CHUNK - N_META
M_QK_W = M_HEADS * M_DK
M_V_W = M_HEADS * M_DV
M_MAIN_W = 2 * M_QK_W + 2 * M_V_W
D_HEADS = 8
D_HD = 64
D_QK_W = D_HEADS * 2 * D_HD
D_V_W = D_HEADS * 2 * D_HD
ROPE_THETA = 500000.0
ROPE_DIM = D_HD // 4
FF = 2816
TAIL = M_CHUNK
LANES = 128
VMEM_LIMIT = 56 * 1024 * 1024


def _params(*sem):
    return pltpu.CompilerParams(dimension_semantics=sem, vmem_limit_bytes=VMEM_LIMIT)


def _row_tile(rows, target):
    best = 16
    for t in range(16, target + 1, 16):
        if rows % t == 0:
            best = t
    return best


def _rms_rows(x, g):
    ms = jnp.mean(x * x, axis=-1, keepdims=True)
    return x * lax.rsqrt(ms + RMS_EPS) * g


def _norm_mm_kernel(x_ref, g_ref, w_ref, b_ref, o_ref, xn_ref):
    @pl.when(pl.program_id(1) == 0)
    def _():
        xn_ref[...] = _rms_rows(x_ref[...], g_ref[...]).astype(BF16)

    acc = jnp.dot(xn_ref[...], w_ref[...], preferred_element_type=F32)
    o_ref[...] = (acc + b_ref[...]).astype(o_ref.dtype)


def _norm_mm(x, g, w, b, out_dtype, tn, name):
    rows, d = x.shape
    n = w.shape[1]
    tm = _row_tile(rows, 1024)
    return pl.pallas_call(
        _norm_mm_kernel,
        grid=(rows // tm, n // tn),
        in_specs=[
            pl.BlockSpec((tm, d), lambda i, j: (i, 0)),
            pl.BlockSpec((1, d), lambda i, j: (0, 0)),
            pl.BlockSpec((d, tn), lambda i, j: (0, j)),
            pl.BlockSpec((1, tn), lambda i, j: (0, j)),
        ],
        out_specs=pl.BlockSpec((tm, tn), lambda i, j: (i, j)),
        out_shape=jax.ShapeDtypeStruct((rows, n), out_dtype),
        scratch_shapes=[pltpu.VMEM((tm, d), BF16)],
        compiler_params=_params("arbitrary", "arbitrary"),
        name=name,
    )(x, g, w, b)


def _group_ones(width, group):
    gi = lax.broadcasted_iota(jnp.int32, (width, width), 0) // group
    gj = lax.broadcasted_iota(jnp.int32, (width, width), 1) // group
    return jnp.where(gi == gj, 1.0, 0.0).astype(BF16)


def _qkv_rope_kernel(x_ref, g_ref, w_ref, tab_ref, o_ref):
    xn = _rms_rows(x_ref[...], g_ref[...]).astype(BF16)
    wide = 2 * LANES
    group = _group_ones(wide, D_HD)
    half = ROPE_DIM // 2
    for j in range(3):
        acc = jnp.dot(xn, w_ref[:, j * D_QK_W:(j + 1) * D_QK_W], preferred_element_type=F32)
        if j == 2:
            o_ref[:, j * D_QK_W:(j + 1) * D_QK_W] = acc.astype(BF16)
            continue
        for p in range(D_QK_W // wide):
            x2 = acc[:, p * wide:(p + 1) * wide]
            ssq = jnp.dot((x2 * x2).astype(BF16), group, preferred_element_type=F32)
            xs = x2 * lax.rsqrt(ssq * (1.0 / D_HD) + RMS_EPS)
            for hf in range(2):
                xh = xs[:, hf * LANES:(hf + 1) * LANES]
                out = (xh * tab_ref[j, 0] + pltpu.roll(xh, LANES - half, 1) * tab_ref[j, 1]
                       + pltpu.roll(xh, half, 1) * tab_ref[j, 2])
                c0 = j * D_QK_W + p * wide + hf * LANES
                o_ref[:, c0:c0 + LANES] = out.astype(BF16)


def _qkv_rope(x, g, w, tables, lp, name):
    rows, d = x.shape
    n = w.shape[1]
    tm = _row_tile(lp, 1056)
    nb = lp // tm
    return pl.pallas_call(
        _qkv_rope_kernel,
        grid=(rows // tm,),
        in_specs=[
            pl.BlockSpec((tm, d), lambda i: (i, 0)),
            pl.BlockSpec((1, d), lambda i: (0, 0)),
            pl.BlockSpec((d, n), lambda i: (0, 0), pipeline_mode=pl.Buffered(1)),
            pl.BlockSpec((2, 3, tm, LANES), lambda i: (0, 0, i % nb, 0)),
        ],
        out_specs=pl.BlockSpec((tm, n), lambda i: (i, 0)),
        out_shape=jax.ShapeDtypeStruct((rows, n), BF16),
        compiler_params=_params("arbitrary"),
        name=name,
    )(x, g, w, tables)


def _mm_res_kernel(a_ref, w_ref, r_ref, o_ref):
    o_ref[...] = r_ref[...] + jnp.dot(a_ref[...], w_ref[...], preferred_element_type=F32)


def _mm_res(a, w, r, name):
    rows, k = a.shape
    n = w.shape[1]
    tm = _row_tile(rows, 1024)
    return pl.pallas_call(
        _mm_res_kernel,
        grid=(rows // tm,),
        in_specs=[
            pl.BlockSpec((tm, k), lambda i: (i, 0)),
            pl.BlockSpec((k, n), lambda i: (0, 0)),
            pl.BlockSpec((tm, n), lambda i: (i, 0)),
        ],
        out_specs=pl.BlockSpec((tm, n), lambda i: (i, 0)),
        out_shape=jax.ShapeDtypeStruct((rows, n), F32),
        compiler_params=_params("arbitrary"),
        name=name,
    )(a, w, r)


FF_CHUNK = 512


def _ffn_kernel(x_ref, g_ref, wi_ref, wo_ref, o_ref, xn_ref, acc_ref):
    x = x_ref[...]
    xn_ref[...] = _rms_rows(x, g_ref[...]).astype(BF16)
    acc_ref[...] = x
    for c0 in range(0, FF, FF_CHUNK):
        c1 = min(c0 + FF_CHUNK, FF)
        xn = xn_ref[...]
        gate = jnp.dot(xn, wi_ref[:, c0:c1], preferred_element_type=F32)
        up = jnp.dot(xn, wi_ref[:, FF + c0:FF + c1], preferred_element_type=F32)
        act = (gate * (1.0 / (1.0 + jnp.exp(-gate))) * up).astype(BF16)
        acc_ref[...] += jnp.dot(act, wo_ref[c0:c1, :], preferred_element_type=F32)
    o_ref[...] = acc_ref[...]


def _ffn(x, g, wi, wo, name):
    rows, d = x.shape
    tm = _row_tile(rows, 512)
    return pl.pallas_call(
        _ffn_kernel,
        grid=(rows // tm,),
        in_specs=[
            pl.BlockSpec((tm, d), lambda i: (i, 0)),
            pl.BlockSpec((1, d), lambda i: (0, 0)),
            pl.BlockSpec((d, 2 * FF), lambda i: (0, 0), pipeline_mode=pl.Buffered(1)),
            pl.BlockSpec((FF, d), lambda i: (0, 0), pipeline_mode=pl.Buffered(1)),
        ],
        out_specs=pl.BlockSpec((tm, d), lambda i: (i, 0)),
        out_shape=jax.ShapeDtypeStruct((rows, d), F32),
        scratch_shapes=[pltpu.VMEM((tm, d), BF16), pltpu.VMEM((tm, d), F32)],
        compiler_params=_params("arbitrary"),
        name=name,
    )(x, g, wi, wo)


def _split3(x):
    hi = x.astype(BF16)
    r1 = x - hi.astype(F32)
    mid = r1.astype(BF16)
    lo = (r1 - mid.astype(F32)).astype(BF16)
    return hi, mid, lo


def _dot_exact01(m01, x, dims):
    out = None
    for part in _split3(x):
        t = lax.dot_general(m01, part, dims, preferred_element_type=F32)
        out = t if out is None else out + t
    return out


_NN = (((1,), (0,)), ((), ()))
_NT = (((1,), (1,)), ((), ()))
_TN = (((0,), (0,)), ((), ()))


def _gate_prep_kernel(g_ref, row_ref, col_ref, *, seq):
    rb = M_XCHUNK
    r = pl.program_id(1) * rb + lax.broadcasted_iota(jnp.int32, (rb, LANES), 0)
    valid = jnp.logical_or(r < seq, jnp.logical_and(r >= seq + M_PAD, r < seq + TAIL))
    g = jnp.where(valid, g_ref[0], 0.0)
    logf = jnp.where(valid, jnp.minimum(g, 0.0) - jnp.log(1.0 + jnp.exp(-jnp.abs(g))), 0.0)
    ig = jnp.where(valid, g, NEG)
    ti = lax.broadcasted_iota(jnp.int32, (rb, rb), 0)
    si = lax.broadcasted_iota(jnp.int32, (rb, rb), 1)
    pre_m = jnp.where(si <= ti, 1.0, 0.0).astype(BF16)
    suf_m = jnp.where(si >= ti, 1.0, 0.0).astype(BF16)
    pre = _dot_exact01(pre_m, logf, _NN)
    suf = _dot_exact01(suf_m, logf, _NN)
    lane = lax.broadcasted_iota(jnp.int32, (rb, LANES), 1)
    pre_dn = pltpu.roll(pre, LANES - M_HEADS, 1)
    suf_dn = pltpu.roll(suf, LANES - M_HEADS, 1)
    row = jnp.where(lane < 4, ig - pre_dn,
                    jnp.where(lane < 8, pre,
                              jnp.where(lane < 12, ig - suf_dn, suf)))
    row = jnp.where(lane < 16, row, 0.0)
    row_ref[0] = row
    er = lax.broadcasted_iota(jnp.int32, (8, LANES), 0)
    el = lax.broadcasted_iota(jnp.int32, (8, LANES), 1)
    pick = jnp.where(el == jnp.where(er < 4, er, er + 4), 1.0, 0.0).astype(BF16)
    col_ref[0, 0] = _dot_exact01(pick, row, _NT)


def _gate_prep(g, seq, name):
    b, lp, _ = g.shape
    nblk = seq // M_XCHUNK + 1
    return pl.pallas_call(
        functools.partial(_gate_prep_kernel, seq=seq),
        grid=(b, nblk),
        in_specs=[pl.BlockSpec((1, M_XCHUNK, LANES), lambda i, j: (i, j, 0))],
        out_specs=[
            pl.BlockSpec((1, M_XCHUNK, LANES), lambda i, j: (i, j, 0)),
            pl.BlockSpec((1, 1, 8, M_XCHUNK), lambda i, j: (i, j, 0, 0)),
        ],
        out_shape=[
            jax.ShapeDtypeStruct((b, lp, LANES), F32),
            jax.ShapeDtypeStruct((b, nblk, 8, M_XCHUNK), F32),
        ],
        compiler_params=_params("arbitrary", "arbitrary"),
        name=name,
    )(g)


def _lane_pick(x, idx):
    lane = lax.broadcasted_iota(jnp.int32, x.shape, 1)
    return jnp.sum(jnp.where(lane == idx, x, 0.0), axis=1, keepdims=True)


def _mlstm_kernel(q_ref, k_ref, v_ref, o_ref, gr_ref, gc_ref, gh_ref, out_ref,
                  hf_ref, hb_ref, ct_ref, n_ref, m_ref, *, seq):
    head = pl.program_id(1)
    nx = seq // M_XCHUNK
    qk_scale = M_DK ** -0.5
    ct_ref[...] = jnp.zeros_like(ct_ref)
    n_ref[...] = jnp.zeros_like(n_ref)
    m_ref[...] = jnp.zeros_like(m_ref)

    def step(d, c, size):
        rows = pl.ds(pl.multiple_of(c * M_XCHUNK, M_CHUNK), size)
        ti = lax.broadcasted_iota(jnp.int32, (size, size), 0)
        si = lax.broadcasted_iota(jnp.int32, (size, size), 1)
        q = q_ref[0, rows, :]
        k = k_ref[0, rows, :]
        v = v_ref[0, rows, :]
        grow = gr_ref[0, rows, :]
        alpha_r = _lane_pick(grow, 8 * d + head)
        beta_r = _lane_pick(grow, 8 * d + 4 + head)
        alpha_c = gc_ref[0, c, pl.ds(4 * d + head, 1), :][:, :size]
        m_prev = m_ref[d]
        ct = ct_ref[d]
        nvec = n_ref[d]
        keep = (si <= ti) if d == 0 else (si >= ti)
        dmat = jnp.where(keep, beta_r + alpha_c, -jnp.inf)
        m_inter = beta_r + m_prev
        m_comb = jnp.maximum(jnp.max(dmat, axis=1, keepdims=True), m_inter)
        s = lax.dot_general(q, k, _NT, preferred_element_type=F32) * qk_scale
        w = jnp.exp(dmat - m_comb) * s
        scale = jnp.exp(m_inter - m_comb)
        q_c = jnp.dot(q, ct.astype(BF16), preferred_element_type=F32) * qk_scale
        num = jnp.dot(w.astype(BF16), v, preferred_element_type=F32) + scale * q_c
        q_n = jnp.sum(q.astype(F32) * nvec, axis=1, keepdims=True) * qk_scale
        den = jnp.sum(w, axis=1, keepdims=True) + scale * q_n
        hval = num / jnp.maximum(jnp.abs(den), jnp.exp(-m_comb))
        if d == 0:
            hf_ref[rows, :] = hval
            tot = beta_r[size - 1:size]
        else:
            hb_ref[rows, :] = hval
            tot = beta_r[0:1]
        a_max = jnp.max(alpha_c, axis=1, keepdims=True)
        m_new = jnp.maximum(tot + m_prev, tot + a_max)
        decay = jnp.exp(tot + m_prev - m_new)
        wk = jnp.exp(alpha_r + tot - m_new)
        vw = (v.astype(F32) * wk).astype(BF16)
        ct_ref[d] = decay * ct + lax.dot_general(k, vw, _TN, preferred_element_type=F32)
        n_ref[d] = decay * nvec + jnp.sum(k.astype(F32) * wk, axis=0, keepdims=True)
        m_ref[d] = m_new

    step(0, nx, TAIL)

    def body(i, carry):
        step(0, i, M_XCHUNK)
        step(1, nx - 1 - i, M_XCHUNK)
        return carry

    lax.fori_loop(0, nx, body, 0)
    step(1, nx, TAIL)

    def finish(c, size):
        rows = pl.ds(pl.multiple_of(c * M_XCHUNK, M_CHUNK), size)
        hsum = hf_ref[rows, :] + hb_ref[rows, :]
        og = o_ref[0, rows, :].astype(F32)
        out_ref[0, rows, :] = (_rms_rows(hsum, gh_ref[...]) * (1.0 / (1.0 + jnp.exp(-og)))).astype(BF16)

    def finish_body(c, carry):
        finish(c, M_XCHUNK)
        return carry

    lax.fori_loop(0, nx, finish_body, 0)
    finish(nx, TAIL)


def _mlstm(proj, grow, gcol, g_head, seq, name):
    b, lp, _ = proj.shape
    nblk = seq // M_XCHUNK + 1
    return pl.pallas_call(
        functools.partial(_mlstm_kernel, seq=seq),
        grid=(b, M_HEADS),
        in_specs=[
            pl.BlockSpec((1, lp, M_DK), lambda i, h: (i, 0, h)),
            pl.BlockSpec((1, lp, M_DK), lambda i, h: (i, 0, M_HEADS + h)),
            pl.BlockSpec((1, lp, M_DV), lambda i, h: (i, 0, M_HEADS + h)),
            pl.BlockSpec((1, lp, M_DV), lambda i, h: (i, 0, 2 * M_HEADS + h)),
            pl.BlockSpec((1, lp, LANES), lambda i, h: (i, 0, 0)),
            pl.BlockSpec((1, nblk, 8, M_XCHUNK), lambda i, h: (i, 0, 0, 0)),
            pl.BlockSpec((1, M_DV), lambda i, h: (0, 0)),
        ],
        out_specs=pl.BlockSpec((1, lp, M_DV), lambda i, h: (i, 0, h)),
        out_shape=jax.ShapeDtypeStruct((b, lp, M_V_W), BF16),
        scratch_shapes=[
            pltpu.VMEM((lp, M_DV), F32),
            pltpu.VMEM((lp, M_DV), F32),
            pltpu.VMEM((2, M_DK, M_DV), F32),
            pltpu.VMEM((2, 1, M_DK), F32),
            pltpu.VMEM((2, 1, 1), F32),
        ],
        compiler_params=_params("arbitrary", "arbitrary"),
        name=name,
    )(proj, proj, proj, proj, grow, gcol, g_head)


ATTN_TK = 2048


ATTN_TQ = 1056


def _attn_kernel(q_ref, k_ref, v_ref, lam_ref, gs_ref, out_ref, *, seq, tq, lambda_init):
    lp = seq + TAIL
    tk = min(ATTN_TK, seq)
    q = q_ref[0]
    lane = lax.broadcasted_iota(jnp.int32, (tq, LANES), 1)
    zero = jnp.zeros_like(q)
    qs = (jnp.where(lane < D_HD, q, zero), jnp.where(lane >= D_HD, q, zero))

    kt = k_ref[0, seq:lp, :]
    vt = v_ref[0, seq:lp, :]
    tail_ok = lax.broadcasted_iota(jnp.int32, (tq, TAIL), 1) >= M_PAD
    state = []
    for qc in qs:
        s = jnp.where(tail_ok, lax.dot_general(qc, kt, _NT, preferred_element_type=F32), NEG)
        m = jnp.max(s, axis=1, keepdims=True)
        e = jnp.exp2(s - m)
        state += [m, jnp.sum(e, axis=1, keepdims=True),
                  jnp.dot(e.astype(BF16), vt, preferred_element_type=F32)]

    def body(j, st):
        rows = pl.ds(pl.multiple_of(j * tk, tk), tk)
        kc = k_ref[0, rows, :]
        vc = v_ref[0, rows, :]
        new = []
        for c, qc in enumerate(qs):
            m, l, acc = st[3 * c:3 * c + 3]
            s = lax.dot_general(qc, kc, _NT, preferred_element_type=F32)
            m_new = jnp.maximum(m, jnp.max(s, axis=1, keepdims=True))
            a = jnp.exp2(m - m_new)
            e = jnp.exp2(s - m_new)
            new += [m_new, a * l + jnp.sum(e, axis=1, keepdims=True),
                    a * acc + jnp.dot(e.astype(BF16), vc, preferred_element_type=F32)]
        return tuple(new)

    m1, l1, acc1, m2, l2, acc2 = lax.fori_loop(0, seq // tk, body, tuple(state))
    lam_p = lam_ref[...]
    lam = (jnp.exp(jnp.sum(lam_p[0:1] * lam_p[1:2], axis=1, keepdims=True))
           - jnp.exp(jnp.sum(lam_p[2:3] * lam_p[3:4], axis=1, keepdims=True)) + lambda_init)
    o = acc1 / l1 - lam * (acc2 / l2)
    out_ref[0] = (_rms_rows(o, gs_ref[...]) * (1.0 - lambda_init)).astype(BF16)


def _attn(proj, lam, gs, seq, lambda_init, name):
    b, lp, _ = proj.shape
    tq = _row_tile(lp, ATTN_TQ)
    full = lambda i, h, j: (0, 0)
    return pl.pallas_call(
        functools.partial(_attn_kernel, seq=seq, tq=tq, lambda_init=lambda_init),
        grid=(b, D_HEADS, lp // tq),
        in_specs=[
            pl.BlockSpec((1, tq, LANES), lambda i, h, j: (i, j, h)),
            pl.BlockSpec((1, lp, LANES), lambda i, h, j: (i, 0, D_HEADS + h)),
            pl.BlockSpec((1, lp, LANES), lambda i, h, j: (i, 0, 2 * D_HEADS + h)),
            pl.BlockSpec((4, D_HD), full),
            pl.BlockSpec((1, LANES), full),
        ],
        out_specs=pl.BlockSpec((1, tq, LANES), lambda i, h, j: (i, j, h)),
        out_shape=jax.ShapeDtypeStruct((b, lp, D_V_W), BF16),
        compiler_params=_params("arbitrary", "arbitrary", "arbitrary"),
        name=name,
    )(proj, proj, proj, lam, gs)


def _rope_tables(seq, g_q, g_k):
    lp = seq + TAIL
    r = jnp.arange(lp)
    pos = jnp.where(r < seq, r + N_META, jnp.maximum(r - seq - M_PAD, 0)).astype(F32)
    half = ROPE_DIM // 2
    inv = ROPE_THETA ** (-jnp.arange(0, ROPE_DIM, 2, dtype=F32) / ROPE_DIM)
    ang = pos[:, None] * inv[None, :]
    cos, sin = jnp.cos(ang), jnp.sin(ang)
    rest = D_HD - ROPE_DIM

    def fold(g, scale):
        g = g.astype(F32) * scale
        cos_t = jnp.concatenate([cos * g[:half], cos * g[half:ROPE_DIM],
                                 jnp.broadcast_to(g[ROPE_DIM:], (lp, rest))], axis=1)
        sin_dn = jnp.concatenate([-sin * g[half:ROPE_DIM], jnp.zeros((lp, half + rest), F32)], axis=1)
        sin_up = jnp.concatenate([jnp.zeros((lp, half), F32), sin * g[:half],
                                  jnp.zeros((lp, rest), F32)], axis=1)
        return jnp.stack([jnp.tile(t, (1, 2)) for t in (cos_t, sin_dn, sin_up)])

    return jnp.stack([fold(g_q, D_HD ** -0.5 * math.log2(math.e)), fold(g_k, 1.0)])


def _trunk(x, meta_tokens, norm_mix, norm_ffn, m_w_main, m_w_gate, m_b_gate, m_g_head, m_w_out,
           d_w_in, d_g_q, d_g_k, d_lam, d_g_sub, d_w_out, ffn_w_in, ffn_w_out, tag):
    b, seq, d = x.shape
    lp = seq + TAIL
    rows = b * lp
    meta = jnp.broadcast_to(meta_tokens[None].astype(x.dtype), (b, N_META, d))
    h = jnp.concatenate([x, jnp.zeros((b, M_PAD, d), x.dtype), meta], axis=1).reshape(rows, d)
    zero_main = jnp.zeros((1, M_MAIN_W), F32)
    for i in range(DEPTH):
        j = i // 2
        gmix = norm_mix[i][None]
        if i % 2 == 0:
            proj = _norm_mm(h, gmix, m_w_main[j], zero_main, BF16, 1024, f"m_in_{tag}{i}")
            gates = _norm_mm(h, gmix, m_w_gate[j], m_b_gate[j], F32, LANES, f"m_gate_{tag}{i}")
            grow, gcol = _gate_prep(gates.reshape(b, lp, LANES), seq, f"m_prep_{tag}{i}")
            mixed = _mlstm(proj.reshape(b, lp, M_MAIN_W), grow, gcol, m_g_head[j][None], seq,
                           f"m_cell_{tag}{i}")
            h = _mm_res(mixed.reshape(rows, M_V_W), m_w_out[j], h, f"m_out_{tag}{i}")
        else:
            lambda_init = 0.8 - 0.6 * math.exp(-0.3 * i)
            tables = _rope_tables(seq, d_g_q[j], d_g_k[j])
            proj = _qkv_rope(h, gmix, d_w_in[j], tables, lp, f"d_in_{tag}{i}")
            mixed = _attn(proj.reshape(b, lp, 3 * D_QK_W), d_lam[j], d_g_sub[j], seq, lambda_init,
                          f"d_attn_{tag}{i}")
            h = _mm_res(mixed.reshape(rows, D_V_W), d_w_out[j], h, f"d_out_{tag}{i}")
        h = _ffn(h, norm_ffn[i][None], ffn_w_in[i], ffn_w_out[i], f"ffn_{tag}{i}")
    return h.reshape(b, lp, d)[:, :seq]


def kernel(x_prompt, x_sample, meta_tokens, norm_mix, norm_ffn, m_w_in, m_b_gate, m_g_head, m_w_out,
           d_w_in, d_g_q, d_g_k, d_lam_q1, d_lam_k1, d_lam_q2, d_lam_k2, d_g_sub, d_w_out,
           ffn_w_in, ffn_w_out):
    n_gate = 4 * M_HEADS
    m_w_main = m_w_in[:, :, :M_MAIN_W].astype(BF16)
    m_w_gate = jnp.pad(m_w_in[:, :, M_MAIN_W:], ((0, 0), (0, 0), (0, LANES - n_gate))).astype(BF16)
    b_gate = jnp.pad(m_b_gate.astype(F32), ((0, 0), (0, LANES - n_gate)))[:, None, :]
    d_lam = jnp.stack([d_lam_q1, d_lam_k1, d_lam_q2, d_lam_k2], axis=1).astype(F32)
    args = (meta_tokens, norm_mix.astype(F32), norm_ffn.astype(F32), m_w_main, m_w_gate, b_gate,
            m_g_head.astype(F32), m_w_out.astype(BF16), d_w_in.astype(BF16), d_g_q, d_g_k,
            d_lam, d_g_sub.astype(F32)[:, None, :], d_w_out.astype(BF16), ffn_w_in.astype(BF16),
            ffn_w_out.astype(BF16))
    return (_trunk(x_prompt, *args, "p"), _trunk(x_sample, *args, "s"))
```

```python
import functools
import math

import jax
import jax.numpy as jnp
from jax import lax
from jax.experimental import pallas as pl
from jax.experimental.pallas import tpu as pltpu

F32 = jnp.float32
BF16 = jnp.bfloat16

D_MODEL = 1024
DEPTH = 4
N_META = 16
RMS_EPS = 1e-6
NEG = -1e30
M_HEADS = 4
M_DK = 128
M_DV = 256
M_CHUNK = 64
M_XCHUNK = 256
M_PAD = M_CHUNK - N_META
M_QK_W = M_HEADS * M_DK
M_V_W = M_HEADS * M_DV
M_MAIN_W = 2 * M_QK_W + 2 * M_V_W
D_HEADS = 8
D_HD = 64
D_QK_W = D_HEADS * 2 * D_HD
D_V_W = D_HEADS * 2 * D_HD
ROPE_THETA = 500000.0
ROPE_DIM = D_HD // 4
FF = 2816
TAIL = M_CHUNK
LANES = 128
VMEM_LIMIT = 56 * 1024 * 1024


def _params(*sem):
    return pltpu.CompilerParams(dimension_semantics=sem, vmem_limit_bytes=VMEM_LIMIT)


def _row_tile(rows, target):
    best = 16
    for t in range(16, target + 1, 16):
        if rows % t == 0:
            best = t
    return best


def _rms_rows(x, g):
    ms = jnp.mean(x * x, axis=-1, keepdims=True)
    return x * lax.rsqrt(ms + RMS_EPS) * g


MM_COLS = 1024


def _m_in_kernel(x_ref, g_ref, w_ref, wg_ref, bg_ref, o_ref, og_ref):
    xn = _rms_rows(x_ref[...], g_ref[...]).astype(BF16)
    for c0 in range(0, M_MAIN_W, MM_COLS):
        o_ref[:, c0:c0 + MM_COLS] = jnp.dot(
            xn, w_ref[:, c0:c0 + MM_COLS], preferred_element_type=F32).astype(BF16)
    og_ref[...] = jnp.dot(xn, wg_ref[...], preferred_element_type=F32) + bg_ref[...]


def _m_in(x, g, w, wg, bg, name):
    rows, d = x.shape
    tm = _row_tile(rows, 1056)
    const = lambda i: (0, 0)
    return pl.pallas_call(
        _m_in_kernel,
        grid=(rows // tm,),
        in_specs=[
            pl.BlockSpec((tm, d), lambda i: (i, 0)),
            pl.BlockSpec((1, d), const),
            pl.BlockSpec((d, M_MAIN_W), const, pipeline_mode=pl.Buffered(1)),
            pl.BlockSpec((d, LANES), const, pipeline_mode=pl.Buffered(1)),
            pl.BlockSpec((1, LANES), const),
        ],
        out_specs=[pl.BlockSpec((tm, M_MAIN_W), lambda i: (i, 0)),
                   pl.BlockSpec((tm, LANES), lambda i: (i, 0))],
        out_shape=[jax.ShapeDtypeStruct((rows, M_MAIN_W), BF16),
                   jax.ShapeDtypeStruct((rows, LANES), F32)],
        compiler_params=_params("arbitrary"),
        name=name,
    )(x, g, w, wg, bg)


def _group_ones(width, group):
    gi = lax.broadcasted_iota(jnp.int32, (width, width), 0) // group
    gj = lax.broadcasted_iota(jnp.int32, (width, width), 1) // group
    return jnp.where(gi == gj, 1.0, 0.0).astype(BF16)


def _qkv_rope_kernel(x_ref, g_ref, w_ref, tab_ref, o_ref):
    xn = _rms_rows(x_ref[...], g_ref[...]).astype(BF16)
    wide = 2 * LANES
    group = _group_ones(wide, D_HD)
    half = ROPE_DIM // 2
    for j in range(3):
        acc = jnp.dot(xn, w_ref[:, j * D_QK_W:(j + 1) * D_QK_W], preferred_element_type=F32)
        if j == 2:
            o_ref[:, j * D_QK_W:(j + 1) * D_QK_W] = acc.astype(BF16)
            continue
        for p in range(D_QK_W // wide):
            x2 = acc[:, p * wide:(p + 1) * wide]
            ssq = jnp.dot((x2 * x2).astype(BF16), group, preferred_element_type=F32)
            xs = x2 * lax.rsqrt(ssq * (1.0 / D_HD) + RMS_EPS)
            for hf in range(2):
                xh = xs[:, hf * LANES:(hf + 1) * LANES]
                out = (xh * tab_ref[j, 0] + pltpu.roll(xh, LANES - half, 1) * tab_ref[j, 1]
                       + pltpu.roll(xh, half, 1) * tab_ref[j, 2])
                c0 = j * D_QK_W + p * wide + hf * LANES
                o_ref[:, c0:c0 + LANES] = out.astype(BF16)


def _qkv_rope(x, g, w, tables, lp, name):
    rows, d = x.shape
    n = w.shape[1]
    tm = _row_tile(lp, 1056)
    nb = lp // tm
    return pl.pallas_call(
        _qkv_rope_kernel,
        grid=(rows // tm,),
        in_specs=[
            pl.BlockSpec((tm, d), lambda i: (i, 0)),
            pl.BlockSpec((1, d), lambda i: (0, 0)),
            pl.BlockSpec((d, n), lambda i: (0, 0), pipeline_mode=pl.Buffered(1)),
            pl.BlockSpec((2, 3, tm, LANES), lambda i: (0, 0, i % nb, 0)),
        ],
        out_specs=pl.BlockSpec((tm, n), lambda i: (i, 0)),
        out_shape=jax.ShapeDtypeStruct((rows, n), BF16),
        compiler_params=_params("arbitrary"),
        name=name,
    )(x, g, w, tables)


FF_CHUNK = 512
FFN_ROWS = 512


def _ffn_kernel(a_ref, wm_ref, h_ref, g_ref, wi_ref, wo_ref, o_ref, xn_ref, acc_ref):
    x = h_ref[...] + jnp.dot(a_ref[...], wm_ref[...], preferred_element_type=F32)
    xn_ref[...] = _rms_rows(x, g_ref[...]).astype(BF16)
    acc_ref[...] = x
    for c0 in range(0, FF, FF_CHUNK):
        c1 = min(c0 + FF_CHUNK, FF)
        xn = xn_ref[...]
        gate = jnp.dot(xn, wi_ref[:, c0:c1], preferred_element_type=F32)
        up = jnp.dot(xn, wi_ref[:, FF + c0:FF + c1], preferred_element_type=F32)
        act = (gate * (1.0 / (1.0 + jnp.exp(-gate))) * up).astype(BF16)
        acc_ref[...] += jnp.dot(act, wo_ref[c0:c1, :], preferred_element_type=F32)
    o_ref[...] = acc_ref[...]


def _ffn(a, wm, h, g, wi, wo, name, out_seq=None):
    b, lp, d = h.shape
    const = lambda *_: (0, 0)
    weights = [
        pl.BlockSpec((d, d), const, pipeline_mode=pl.Buffered(1)),
        pl.BlockSpec((1, d), const),
        pl.BlockSpec((d, 2 * FF), const, pipeline_mode=pl.Buffered(1)),
        pl.BlockSpec((FF, d), const, pipeline_mode=pl.Buffered(1)),
    ]
    if out_seq is None:
        rows = b * lp
        tm = _row_tile(rows, FFN_ROWS)
        grid = (rows // tm,)
        tile = pl.BlockSpec((tm, d), lambda i: (i, 0))
        a, h = a.reshape(rows, d), h.reshape(rows, d)
        out_shape = jax.ShapeDtypeStruct((rows, d), F32)
        sem = ("arbitrary",)
    else:
        tm = _row_tile(out_seq, FFN_ROWS)
        grid = (b, out_seq // tm)
        tile = pl.BlockSpec((None, tm, d), lambda i, j: (i, j, 0))
        out_shape = jax.ShapeDtypeStruct((b, out_seq, d), F32)
        sem = ("arbitrary", "arbitrary")
    out = pl.pallas_call(
        _ffn_kernel,
        grid=grid,
        in_specs=[tile, weights[0], tile, weights[1], weights[2], weights[3]],
        out_specs=tile,
        out_shape=out_shape,
        scratch_shapes=[pltpu.VMEM((tm, d), BF16), pltpu.VMEM((tm, d), F32)],
        compiler_params=_params(*sem),
        name=name,
    )(a, wm, h, g, wi, wo)
    return out if out_seq is not None else out.reshape(b, lp, d)


def _split3(x):
    hi = x.astype(BF16)
    r1 = x - hi.astype(F32)
    mid = r1.astype(BF16)
    lo = (r1 - mid.astype(F32)).astype(BF16)
    return hi, mid, lo


def _dot_exact01(m01, x, dims):
    out = None
    for part in _split3(x):
        t = lax.dot_general(m01, part, dims, preferred_element_type=F32)
        out = t if out is None else out + t
    return out


_NN = (((1,), (0,)), ((), ()))
_NT = (((1,), (1,)), ((), ()))
_TN = (((0,), (0,)), ((), ()))


def _gate_prep_kernel(g_ref, row_ref, col_ref, *, seq):
    rb = M_XCHUNK
    r = pl.program_id(1) * rb + lax.broadcasted_iota(jnp.int32, (rb, LANES), 0)
    valid = jnp.logical_or(r < seq, jnp.logical_and(r >= seq + M_PAD, r < seq + TAIL))
    g = jnp.where(valid, g_ref[0], 0.0)
    logf = jnp.where(valid, jnp.minimum(g, 0.0) - jnp.log(1.0 + jnp.exp(-jnp.abs(g))), 0.0)
    ig = jnp.where(valid, g, NEG)
    ti = lax.broadcasted_iota(jnp.int32, (rb, rb), 0)
    si = lax.broadcasted_iota(jnp.int32, (rb, rb), 1)
    pre_m = jnp.where(si <= ti, 1.0, 0.0).astype(BF16)
    suf_m = jnp.where(si >= ti, 1.0, 0.0).astype(BF16)
    pre = _dot_exact01(pre_m, logf, _NN)
    suf = _dot_exact01(suf_m, logf, _NN)
    lane = lax.broadcasted_iota(jnp.int32, (rb, LANES), 1)
    pre_dn = pltpu.roll(pre, LANES - M_HEADS, 1)
    suf_dn = pltpu.roll(suf, LANES - M_HEADS, 1)
    row = jnp.where(lane < 4, ig - pre_dn,
                    jnp.where(lane < 8, pre,
                              jnp.where(lane < 12, ig - suf_dn, suf)))
    row = jnp.where(lane < 16, row, 0.0)
    row_ref[0] = row
    er = lax.broadcasted_iota(jnp.int32, (8, LANES), 0)
    el = lax.broadcasted_iota(jnp.int32, (8, LANES), 1)
    pick = jnp.where(el == jnp.where(er < 4, er, er + 4), 1.0, 0.0).astype(BF16)
    col_ref[0, 0] = _dot_exact01(pick, row, _NT)


def _gate_prep(g, seq, name):
    b, lp, _ = g.shape
    nblk = seq // M_XCHUNK + 1
    return pl.pallas_call(
        functools.partial(_gate_prep_kernel, seq=seq),
        grid=(b, nblk),
        in_specs=[pl.BlockSpec((1, M_XCHUNK, LANES), lambda i, j: (i, j, 0))],
        out_specs=[
            pl.BlockSpec((1, M_XCHUNK, LANES), lambda i, j: (i, j, 0)),
            pl.BlockSpec((1, 1, 8, M_XCHUNK), lambda i, j: (i, j, 0, 0)),
        ],
        out_shape=[
            jax.ShapeDtypeStruct((b, lp, LANES), F32),
            jax.ShapeDtypeStruct((b, nblk, 8, M_XCHUNK), F32),
        ],
        compiler_params=_params("arbitrary", "arbitrary"),
        name=name,
    )(g)


def _lane_pick(x, idx):
    lane = lax.broadcasted_iota(jnp.int32, x.shape, 1)
    return jnp.sum(jnp.where(lane == idx, x, 0.0), axis=1, keepdims=True)


def _chunk_start(c):
    return c * M_XCHUNK if isinstance(c, int) else pl.multiple_of(c * M_XCHUNK, M_XCHUNK)


def _mlstm_kernel(q_ref, k_ref, v_ref, o_ref, gr_ref, gc_ref, gh_ref, out_ref,
                  hf_ref, hb_ref, cn_ref, m_ref, *, seq):
    head = pl.program_id(1)
    nx = seq // M_XCHUNK
    qk_scale = M_DK ** -0.5
    cn_ref[...] = jnp.zeros_like(cn_ref)
    m_ref[...] = jnp.zeros_like(m_ref)

    def step(d, c, size):
        rows = pl.ds(_chunk_start(c), size)
        ti = lax.broadcasted_iota(jnp.int32, (size, size), 0)
        si = lax.broadcasted_iota(jnp.int32, (size, size), 1)
        q = q_ref[0, rows, :]
        k = k_ref[0, rows, :]
        v1 = jnp.concatenate([v_ref[0, rows, :], jnp.ones((size, LANES), BF16)], axis=1)
        beta_r = _lane_pick(gr_ref[0, rows, :], 8 * d + 4 + head)
        alpha_c = gc_ref[0, c, pl.ds(4 * d + head, 1), :][:, :size]
        m_prev = m_ref[d]
        cn = cn_ref[d]
        keep = (si <= ti) if d == 0 else (si >= ti)
        dmat = jnp.where(keep, beta_r + alpha_c, -jnp.inf)
        m_inter = beta_r + m_prev
        m_comb = jnp.maximum(jnp.max(dmat, axis=1, keepdims=True), m_inter)
        s = lax.dot_general(q, k, _NT, preferred_element_type=F32)
        w = jnp.exp(dmat - m_comb) * s
        scale = jnp.exp(m_inter - m_comb)
        both = (jnp.dot(w.astype(BF16), v1, preferred_element_type=F32)
                + scale * jnp.dot(q, cn.astype(BF16), preferred_element_type=F32)) * qk_scale
        inv = 1.0 / jnp.maximum(jnp.abs(both[:, M_DV:]), jnp.exp(-m_comb))
        hval = both[:, :M_DV] * jnp.concatenate([inv, inv], axis=1)
        if d == 0:
            hf_ref[rows, :] = hval
            tot = beta_r[size - 1:size]
        else:
            hb_ref[rows, :] = hval
            tot = beta_r[0:1]
        a_max = jnp.max(alpha_c, axis=1, keepdims=True)
        m_new = jnp.maximum(tot + m_prev, tot + a_max)
        decay = jnp.exp(tot + m_prev - m_new)
        wk = jnp.exp(alpha_c + tot - m_new)
        kw = (k.T.astype(F32) * wk).astype(BF16)
        cn_ref[d] = decay * cn + jnp.dot(kw, v1, preferred_element_type=F32)
        m_ref[d] = m_new

    step(0, nx, TAIL)
    step(1, nx - 1, M_XCHUNK)

    def body(i, carry):
        step(0, i - 1, M_XCHUNK)
        step(1, nx - 1 - i, M_XCHUNK)
        return carry

    lax.fori_loop(1, nx, body, 0)
    step(0, nx - 1, M_XCHUNK)
    step(1, nx, TAIL)

    def finish(c, size):
        rows = pl.ds(_chunk_start(c), size)
        hsum = hf_ref[rows, :] + hb_ref[rows, :]
        og = o_ref[0, rows, :].astype(F32)
        out_ref[0, rows, :] = (_rms_rows(hsum, gh_ref[...]) * (1.0 / (1.0 + jnp.exp(-og)))).astype(BF16)

    def finish_body(c, carry):
        finish(c, M_XCHUNK)
        return carry

    lax.fori_loop(0, nx, finish_body, 0)
    finish(nx, TAIL)


def _mlstm(proj, grow, gcol, g_head, seq, name):
    b, lp, _ = proj.shape
    nblk = seq // M_XCHUNK + 1
    return pl.pallas_call(
        functools.partial(_mlstm_kernel, seq=seq),
        grid=(b, M_HEADS),
        in_specs=[
            pl.BlockSpec((1, lp, M_DK), lambda i, h: (i, 0, h)),
            pl.BlockSpec((1, lp, M_DK), lambda i, h: (i, 0, M_HEADS + h)),
            pl.BlockSpec((1, lp, M_DV), lambda i, h: (i, 0, M_HEADS + h)),
            pl.BlockSpec((1, lp, M_DV), lambda i, h: (i, 0, 2 * M_HEADS + h)),
            pl.BlockSpec((1, lp, LANES), lambda i, h: (i, 0, 0)),
            pl.BlockSpec((1, nblk, 8, M_XCHUNK), lambda i, h: (i, 0, 0, 0)),
            pl.BlockSpec((1, M_DV), lambda i, h: (0, 0)),
        ],
        out_specs=pl.BlockSpec((1, lp, M_DV), lambda i, h: (i, 0, h)),
        out_shape=jax.ShapeDtypeStruct((b, lp, M_V_W), BF16),
        scratch_shapes=[
            pltpu.VMEM((lp, M_DV), F32),
            pltpu.VMEM((lp, M_DV), F32),
            pltpu.VMEM((2, M_DK, M_DV + LANES), F32),
            pltpu.VMEM((2, 1, 1), F32),
        ],
        compiler_params=_params("arbitrary", "arbitrary"),
        name=name,
    )(proj, proj, proj, proj, grow, gcol, g_head)


ATTN_TK = 2048


ATTN_TQ = 1056


def _attn_kernel(q_ref, k_ref, v_ref, lam_ref, gs_ref, out_ref, *, seq, tq, lambda_init):
    lp = seq + TAIL
    tk = min(ATTN_TK, seq)
    q = q_ref[0]
    lane = lax.broadcasted_iota(jnp.int32, (tq, LANES), 1)
    zero = jnp.zeros_like(q)
    qs = (jnp.where(lane < D_HD, q, zero), jnp.where(lane >= D_HD, q, zero))

    kt = k_ref[0, seq:lp, :]
    vt = v_ref[0, seq:lp, :]
    tail_ok = lax.broadcasted_iota(jnp.int32, (tq, TAIL), 1) >= M_PAD
    state = []
    for qc in qs:
        s = jnp.where(tail_ok, lax.dot_general(qc, kt, _NT, preferred_element_type=F32), NEG)
        m = jnp.max(s, axis=1, keepdims=True)
        e = jnp.exp2(s - m)
        state += [m, jnp.sum(e, axis=1, keepdims=True),
                  jnp.dot(e.astype(BF16), vt, preferred_element_type=F32)]

    def body(j, st):
        rows = pl.ds(pl.multiple_of(j * tk, tk), tk)
        kc = k_ref[0, rows, :]
        vc = v_ref[0, rows, :]
        new = []
        for c, qc in enumerate(qs):
            m, l, acc = st[3 * c:3 * c + 3]
            s = lax.dot_general(qc, kc, _NT, preferred_element_type=F32)
            m_new = jnp.maximum(m, jnp.max(s, axis=1, keepdims=True))
            a = jnp.exp2(m - m_new)
            e = jnp.exp2(s - m_new)
            new += [m_new, a * l + jnp.sum(e, axis=1, keepdims=True),
                    a * acc + jnp.dot(e.astype(BF16), vc, preferred_element_type=F32)]
        return tuple(new)

    m1, l1, acc1, m2, l2, acc2 = lax.fori_loop(0, seq // tk, body, tuple(state))
    lam_p = lam_ref[...]
    lam = (jnp.exp(jnp.sum(lam_p[0:1] * lam_p[1:2], axis=1, keepdims=True))
           - jnp.exp(jnp.sum(lam_p[2:3] * lam_p[3:4], axis=1, keepdims=True)) + lambda_init)
    o = acc1 / l1 - lam * (acc2 / l2)
    out_ref[0] = (_rms_rows(o, gs_ref[...]) * (1.0 - lambda_init)).astype(BF16)


def _attn(proj, lam, gs, seq, lambda_init, name):
    b, lp, _ = proj.shape
    tq = _row_tile(lp, ATTN_TQ)
    full = lambda i, h, j: (0, 0)
    return pl.pallas_call(
        functools.partial(_attn_kernel, seq=seq, tq=tq, lambda_init=lambda_init),
        grid=(b, D_HEADS, lp // tq),
        in_specs=[
            pl.BlockSpec((1, tq, LANES), lambda i, h, j: (i, j, h)),
            pl.BlockSpec((1, lp, LANES), lambda i, h, j: (i, 0, D_HEADS + h)),
            pl.BlockSpec((1, lp, LANES), lambda i, h, j: (i, 0, 2 * D_HEADS + h)),
            pl.BlockSpec((4, D_HD), full),
            pl.BlockSpec((1, LANES), full),
        ],
        out_specs=pl.BlockSpec((1, tq, LANES), lambda i, h, j: (i, j, h)),
        out_shape=jax.ShapeDtypeStruct((b, lp, D_V_W), BF16),
        compiler_params=_params("arbitrary", "arbitrary", "arbitrary"),
        name=name,
    )(proj, proj, proj, lam, gs)


def _rope_tables(seq, g_q, g_k):
    lp = seq + TAIL
    r = jnp.arange(lp)
    pos = jnp.where(r < seq, r + N_META, jnp.maximum(r - seq - M_PAD, 0)).astype(F32)
    half = ROPE_DIM // 2
    inv = ROPE_THETA ** (-jnp.arange(0, ROPE_DIM, 2, dtype=F32) / ROPE_DIM)
    ang = pos[:, None] * inv[None, :]
    cos, sin = jnp.cos(ang), jnp.sin(ang)
    rest = D_HD - ROPE_DIM

    def fold(g, scale):
        g = g.astype(F32) * scale
        cos_t = jnp.concatenate([cos * g[:half], cos * g[half:ROPE_DIM],
                                 jnp.broadcast_to(g[ROPE_DIM:], (lp, rest))], axis=1)
        sin_dn = jnp.concatenate([-sin * g[half:ROPE_DIM], jnp.zeros((lp, half + rest), F32)], axis=1)
        sin_up = jnp.concatenate([jnp.zeros((lp, half), F32), sin * g[:half],
                                  jnp.zeros((lp, rest), F32)], axis=1)
        return jnp.stack([jnp.tile(t, (1, 2)) for t in (cos_t, sin_dn, sin_up)])

    return jnp.stack([fold(g_q, D_HD ** -0.5 * math.log2(math.e)), fold(g_k, 1.0)])


def _trunk(x, meta_tokens, norm_mix, norm_ffn, m_w_main, m_w_gate, m_b_gate, m_g_head, m_w_out,
           d_w_in, d_g_q, d_g_k, d_lam, d_g_sub, d_w_out, ffn_w_in, ffn_w_out, tag):
    b, seq, d = x.shape
    lp = seq + TAIL
    rows = b * lp
    meta = jnp.broadcast_to(meta_tokens[None].astype(x.dtype), (b, N_META, d))
    h = jnp.concatenate([x, jnp.zeros((b, M_PAD, d), x.dtype), meta], axis=1)
    for i in range(DEPTH):
        j = i // 2
        gmix = norm_mix[i][None]
        hrows = h.reshape(rows, d)
        if i % 2 == 0:
            proj, gates = _m_in(hrows, gmix, m_w_main[j], m_w_gate[j], m_b_gate[j], f"m_in_{tag}{i}")
            grow, gcol = _gate_prep(gates.reshape(b, lp, LANES), seq, f"m_prep_{tag}{i}")
            mixed = _mlstm(proj.reshape(b, lp, M_MAIN_W), grow, gcol, m_g_head[j][None], seq,
                           f"m_cell_{tag}{i}")
            w_mix_out = m_w_out[j]
        else:
            lambda_init = 0.8 - 0.6 * math.exp(-0.3 * i)
            tables = _rope_tables(seq, d_g_q[j], d_g_k[j])
            proj = _qkv_rope(hrows, gmix, d_w_in[j], tables, lp, f"d_in_{tag}{i}")
            mixed = _attn(proj.reshape(b, lp, 3 * D_QK_W), d_lam[j], d_g_sub[j], seq, lambda_init,
                          f"d_attn_{tag}{i}")
            w_mix_out = d_w_out[j]
        h = _ffn(mixed, w_mix_out, h, norm_ffn[i][None], ffn_w_in[i], ffn_w_out[i], f"ffn_{tag}{i}",
                 out_seq=seq if i == DEPTH - 1 else None)
    return h


def kernel(x_prompt, x_sample, meta_tokens, norm_mix, norm_ffn, m_w_in, m_b_gate, m_g_head, m_w_out,
           d_w_in, d_g_q, d_g_k, d_lam_q1, d_lam_k1, d_lam_q2, d_lam_k2, d_g_sub, d_w_out,
           ffn_w_in, ffn_w_out):
    n_gate = 4 * M_HEADS
    m_w_main = m_w_in[:, :, :M_MAIN_W].astype(BF16)
    m_w_gate = jnp.pad(m_w_in[:, :, M_MAIN_W:], ((0, 0), (0, 0), (0, LANES - n_gate))).astype(BF16)
    b_gate = jnp.pad(m_b_gate.astype(F32), ((0, 0), (0, LANES - n_gate)))[:, None, :]
    d_lam = jnp.stack([d_lam_q1, d_lam_k1, d_lam_q2, d_lam_k2], axis=1).astype(F32)
    args = (meta_tokens, norm_mix.astype(F32), norm_ffn.astype(F32), m_w_main, m_w_gate, b_gate,
            m_g_head.astype(F32), m_w_out.astype(BF16), d_w_in.astype(BF16), d_g_q, d_g_k,
            d_lam, d_g_sub.astype(F32)[:, None, :], d_w_out.astype(BF16), ffn_w_in.astype(BF16),
            ffn_w_out.astype(BF16))
    return (_trunk(x_prompt, *args, "p"), _trunk(x_sample, *args, "s"))
```

```python
import functools
import math

import jax
import jax.numpy as jnp
from jax import lax
from jax.experimental import pallas as pl
from jax.experimental.pallas import tpu as pltpu

F32 = jnp.float32
BF16 = jnp.bfloat16

D_MODEL = 1024
DEPTH = 4
N_META = 16
RMS_EPS = 1e-6
NEG = -1e30
M_HEADS = 4
M_DK = 128
M_DV = 256
M_CHUNK = 64
M_XCHUNK = 256
M_PAD = M_CHUNK - N_META
M_QK_W = M_HEADS * M_DK
M_V_W = M_HEADS * M_DV
M_MAIN_W = 2 * M_QK_W + 2 * M_V_W
D_HEADS = 8
D_HD = 64
D_QK_W = D_HEADS * 2 * D_HD
D_V_W = D_HEADS * 2 * D_HD
ROPE_THETA = 500000.0
ROPE_DIM = D_HD // 4
FF = 2816
TAIL = M_CHUNK
LANES = 128
VMEM_LIMIT = 56 * 1024 * 1024


def _params(*sem):
    return pltpu.CompilerParams(dimension_semantics=sem, vmem_limit_bytes=VMEM_LIMIT)


def _row_tile(rows, target):
    best = 16
    for t in range(16, target + 1, 16):
        if rows % t == 0:
            best = t
    return best


def _rms_rows(x, g):
    ms = jnp.mean(x * x, axis=-1, keepdims=True)
    return x * lax.rsqrt(ms + RMS_EPS) * g


MM_COLS = 1024


def _m_in_kernel(x_ref, g_ref, w_ref, wg_ref, bg_ref, o_ref, og_ref):
    xn = _rms_rows(x_ref[...], g_ref[...]).astype(BF16)
    for c0 in range(0, M_MAIN_W, MM_COLS):
        o_ref[:, c0:c0 + MM_COLS] = jnp.dot(
            xn, w_ref[:, c0:c0 + MM_COLS], preferred_element_type=F32).astype(BF16)
    og_ref[...] = jnp.dot(xn, wg_ref[...], preferred_element_type=F32) + bg_ref[...]


def _m_in(x, g, w, wg, bg, name):
    rows, d = x.shape
    tm = _row_tile(rows, 1056)
    const = lambda i: (0, 0)
    return pl.pallas_call(
        _m_in_kernel,
        grid=(rows // tm,),
        in_specs=[
            pl.BlockSpec((tm, d), lambda i: (i, 0)),
            pl.BlockSpec((1, d), const),
            pl.BlockSpec((d, M_MAIN_W), const, pipeline_mode=pl.Buffered(1)),
            pl.BlockSpec((d, LANES), const, pipeline_mode=pl.Buffered(1)),
            pl.BlockSpec((1, LANES), const),
        ],
        out_specs=[pl.BlockSpec((tm, M_MAIN_W), lambda i: (i, 0)),
                   pl.BlockSpec((tm, LANES), lambda i: (i, 0))],
        out_shape=[jax.ShapeDtypeStruct((rows, M_MAIN_W), BF16),
                   jax.ShapeDtypeStruct((rows, LANES), F32)],
        compiler_params=_params("arbitrary"),
        name=name,
    )(x, g, w, wg, bg)


def _group_ones(width, group):
    gi = lax.broadcasted_iota(jnp.int32, (width, width), 0) // group
    gj = lax.broadcasted_iota(jnp.int32, (width, width), 1) // group
    return jnp.where(gi == gj, 1.0, 0.0).astype(BF16)


def _qkv_rope_kernel(x_ref, g_ref, w_ref, tab_ref, o_ref):
    xn = _rms_rows(x_ref[...], g_ref[...]).astype(BF16)
    wide = 2 * LANES
    group = _group_ones(wide, D_HD)
    half = ROPE_DIM // 2
    for j in range(3):
        acc = jnp.dot(xn, w_ref[:, j * D_QK_W:(j + 1) * D_QK_W], preferred_element_type=F32)
        if j == 2:
            o_ref[:, j * D_QK_W:(j + 1) * D_QK_W] = acc.astype(BF16)
            continue
        for p in range(D_QK_W // wide):
            x2 = acc[:, p * wide:(p + 1) * wide]
            ssq = jnp.dot((x2 * x2).astype(BF16), group, preferred_element_type=F32)
            xs = x2 * lax.rsqrt(ssq * (1.0 / D_HD) + RMS_EPS)
            for hf in range(2):
                xh = xs[:, hf * LANES:(hf + 1) * LANES]
                out = (xh * tab_ref[j, 0] + pltpu.roll(xh, LANES - half, 1) * tab_ref[j, 1]
                       + pltpu.roll(xh, half, 1) * tab_ref[j, 2])
                c0 = j * D_QK_W + p * wide + hf * LANES
                o_ref[:, c0:c0 + LANES] = out.astype(BF16)


def _qkv_rope(x, g, w, tables, lp, name):
    rows, d = x.shape
    n = w.shape[1]
    tm = _row_tile(lp, 1056)
    nb = lp // tm
    return pl.pallas_call(
        _qkv_rope_kernel,
        grid=(rows // tm,),
        in_specs=[
            pl.BlockSpec((tm, d), lambda i: (i, 0)),
            pl.BlockSpec((1, d), lambda i: (0, 0)),
            pl.BlockSpec((d, n), lambda i: (0, 0), pipeline_mode=pl.Buffered(1)),
            pl.BlockSpec((2, 3, tm, LANES), lambda i: (0, 0, i % nb, 0)),
        ],
        out_specs=pl.BlockSpec((tm, n), lambda i: (i, 0)),
        out_shape=jax.ShapeDtypeStruct((rows, n), BF16),
        compiler_params=_params("arbitrary"),
        name=name,
    )(x, g, w, tables)


FF_CHUNK = 512
FFN_ROWS = 512


def _ffn_kernel(a_ref, wm_ref, h_ref, g_ref, wi_ref, wo_ref, o_ref, xn_ref, acc_ref):
    x = h_ref[...] + jnp.dot(a_ref[...], wm_ref[...], preferred_element_type=F32)
    xn_ref[...] = _rms_rows(x, g_ref[...]).astype(BF16)
    acc_ref[...] = x
    for c0 in range(0, FF, FF_CHUNK):
        c1 = min(c0 + FF_CHUNK, FF)
        xn = xn_ref[...]
        gate = jnp.dot(xn, wi_ref[:, c0:c1], preferred_element_type=F32)
        up = jnp.dot(xn, wi_ref[:, FF + c0:FF + c1], preferred_element_type=F32)
        act = (gate * (1.0 / (1.0 + jnp.exp(-gate))) * up).astype(BF16)
        acc_ref[...] += jnp.dot(act, wo_ref[c0:c1, :], preferred_element_type=F32)
    o_ref[...] = acc_ref[...]


def _ffn(a, wm, h, g, wi, wo, name, out_seq=None):
    b, lp, d = h.shape
    const = lambda *_: (0, 0)
    weights = [
        pl.BlockSpec((d, d), const, pipeline_mode=pl.Buffered(1)),
        pl.BlockSpec((1, d), const),
        pl.BlockSpec((d, 2 * FF), const, pipeline_mode=pl.Buffered(1)),
        pl.BlockSpec((FF, d), const, pipeline_mode=pl.Buffered(1)),
    ]
    if out_seq is None:
        rows = b * lp
        tm = _row_tile(rows, FFN_ROWS)
        grid = (rows // tm,)
        tile = pl.BlockSpec((tm, d), lambda i: (i, 0))
        a, h = a.reshape(rows, d), h.reshape(rows, d)
        out_shape = jax.ShapeDtypeStruct((rows, d), F32)
        sem = ("arbitrary",)
    else:
        tm = _row_tile(out_seq, FFN_ROWS)
        grid = (b, out_seq // tm)
        tile = pl.BlockSpec((None, tm, d), lambda i, j: (i, j, 0))
        out_shape = jax.ShapeDtypeStruct((b, out_seq, d), F32)
        sem = ("arbitrary", "arbitrary")
    out = pl.pallas_call(
        _ffn_kernel,
        grid=grid,
        in_specs=[tile, weights[0], tile, weights[1], weights[2], weights[3]],
        out_specs=tile,
        out_shape=out_shape,
        scratch_shapes=[pltpu.VMEM((tm, d), BF16), pltpu.VMEM((tm, d), F32)],
        compiler_params=_params(*sem),
        name=name,
    )(a, wm, h, g, wi, wo)
    return out if out_seq is not None else out.reshape(b, lp, d)


def _split3(x):
    hi = x.astype(BF16)
    r1 = x - hi.astype(F32)
    mid = r1.astype(BF16)
    lo = (r1 - mid.astype(F32)).astype(BF16)
    return hi, mid, lo


def _dot_exact01(m01, x, dims):
    out = None
    for part in _split3(x):
        t = lax.dot_general(m01, part, dims, preferred_element_type=F32)
        out = t if out is None else out + t
    return out


_NN = (((1,), (0,)), ((), ()))
_NT = (((1,), (1,)), ((), ()))
_TN = (((0,), (0,)), ((), ()))


def _gate_prep_kernel(g_ref, row_ref, col_ref, *, seq):
    rb = M_XCHUNK
    r = pl.program_id(1) * rb + lax.broadcasted_iota(jnp.int32, (rb, LANES), 0)
    valid = jnp.logical_or(r < seq, jnp.logical_and(r >= seq + M_PAD, r < seq + TAIL))
    g = jnp.where(valid, g_ref[0], 0.0)
    logf = jnp.where(valid, jnp.minimum(g, 0.0) - jnp.log(1.0 + jnp.exp(-jnp.abs(g))), 0.0)
    ig = jnp.where(valid, g, NEG)
    ti = lax.broadcasted_iota(jnp.int32, (rb, rb), 0)
    si = lax.broadcasted_iota(jnp.int32, (rb, rb), 1)
    pre_m = jnp.where(si <= ti, 1.0, 0.0).astype(BF16)
    suf_m = jnp.where(si >= ti, 1.0, 0.0).astype(BF16)
    pre = _dot_exact01(pre_m, logf, _NN)
    suf = _dot_exact01(suf_m, logf, _NN)
    lane = lax.broadcasted_iota(jnp.int32, (rb, LANES), 1)
    pre_dn = pltpu.roll(pre, LANES - M_HEADS, 1)
    suf_dn = pltpu.roll(suf, LANES - M_HEADS, 1)
    row = jnp.where(lane < 4, ig - pre_dn,
                    jnp.where(lane < 8, pre,
                              jnp.where(lane < 12, ig - suf_dn, suf)))
    row = jnp.where(lane < 16, row, 0.0)
    row_ref[0] = row
    er = lax.broadcasted_iota(jnp.int32, (8, LANES), 0)
    el = lax.broadcasted_iota(jnp.int32, (8, LANES), 1)
    pick = jnp.where(el == jnp.where(er < 4, er, er + 4), 1.0, 0.0).astype(BF16)
    col_ref[0, 0] = _dot_exact01(pick, row, _NT)


def _gate_prep(g, seq, name):
    b, lp, _ = g.shape
    nblk = seq // M_XCHUNK + 1
    return pl.pallas_call(
        functools.partial(_gate_prep_kernel, seq=seq),
        grid=(b, nblk),
        in_specs=[pl.BlockSpec((1, M_XCHUNK, LANES), lambda i, j: (i, j, 0))],
        out_specs=[
            pl.BlockSpec((1, M_XCHUNK, LANES), lambda i, j: (i, j, 0)),
            pl.BlockSpec((1, 1, 8, M_XCHUNK), lambda i, j: (i, j, 0, 0)),
        ],
        out_shape=[
            jax.ShapeDtypeStruct((b, lp, LANES), F32),
            jax.ShapeDtypeStruct((b, nblk, 8, M_XCHUNK), F32),
        ],
        compiler_params=_params("arbitrary", "arbitrary"),
        name=name,
    )(g)


def _lane_pick(x, idx):
    lane = lax.broadcasted_iota(jnp.int32, x.shape, 1)
    return jnp.sum(jnp.where(lane == idx, x, 0.0), axis=1, keepdims=True)


def _chunk_start(c):
    return c * M_XCHUNK if isinstance(c, int) else pl.multiple_of(c * M_XCHUNK, M_XCHUNK)


def _mlstm_kernel(q_ref, k_ref, v_ref, o_ref, gr_ref, gc_ref, gh_ref, out_ref,
                  hf_ref, hb_ref, cn_ref, m_ref, *, seq):
    head = pl.program_id(1)
    nx = seq // M_XCHUNK
    qk_scale = M_DK ** -0.5
    cn_ref[...] = jnp.zeros_like(cn_ref)
    m_ref[...] = jnp.zeros_like(m_ref)

    def step(d, c, size):
        rows = pl.ds(_chunk_start(c), size)
        ti = lax.broadcasted_iota(jnp.int32, (size, size), 0)
        si = lax.broadcasted_iota(jnp.int32, (size, size), 1)
        q = q_ref[0, rows, :]
        k = k_ref[0, rows, :]
        v1 = jnp.concatenate([v_ref[0, rows, :], jnp.ones((size, LANES), BF16)], axis=1)
        beta_r = _lane_pick(gr_ref[0, rows, :], 8 * d + 4 + head)
        alpha_c = gc_ref[0, c, pl.ds(4 * d + head, 1), :][:, :size]
        m_prev = m_ref[d]
        cn = cn_ref[d]
        keep = (si <= ti) if d == 0 else (si >= ti)
        dmat = jnp.where(keep, beta_r + alpha_c, -jnp.inf)
        m_inter = beta_r + m_prev
        m_comb = jnp.maximum(jnp.max(dmat, axis=1, keepdims=True), m_inter)
        s = lax.dot_general(q, k, _NT, preferred_element_type=F32)
        w = jnp.exp(dmat - m_comb) * s
        scale = jnp.exp(m_inter - m_comb)
        both = (jnp.dot(w.astype(BF16), v1, preferred_element_type=F32)
                + scale * jnp.dot(q, cn.astype(BF16), preferred_element_type=F32)) * qk_scale
        inv = 1.0 / jnp.maximum(jnp.abs(both[:, M_DV:]), jnp.exp(-m_comb))
        hval = both[:, :M_DV] * jnp.concatenate([inv, inv], axis=1)
        if d == 0:
            hf_ref[rows, :] = hval
            tot = beta_r[size - 1:size]
        else:
            hb_ref[rows, :] = hval
            tot = beta_r[0:1]
        a_max = jnp.max(alpha_c, axis=1, keepdims=True)
        m_new = jnp.maximum(tot + m_prev, tot + a_max)
        decay = jnp.exp(tot + m_prev - m_new)
        wk = jnp.exp(alpha_c + tot - m_new)
        kw = (k.T.astype(F32) * wk).astype(BF16)
        cn_ref[d] = decay * cn + jnp.dot(kw, v1, preferred_element_type=F32)
        m_ref[d] = m_new

    step(0, nx, TAIL)
    step(1, nx - 1, M_XCHUNK)

    def body(i, carry):
        step(0, i - 1, M_XCHUNK)
        step(1, nx - 1 - i, M_XCHUNK)
        return carry

    lax.fori_loop(1, nx, body, 0)
    step(0, nx - 1, M_XCHUNK)
    step(1, nx, TAIL)

    def finish(c, size):
        rows = pl.ds(_chunk_start(c), size)
        hsum = hf_ref[rows, :] + hb_ref[rows, :]
        og = o_ref[0, rows, :].astype(F32)
        out_ref[0, rows, :] = (_rms_rows(hsum, gh_ref[...]) * (1.0 / (1.0 + jnp.exp(-og)))).astype(BF16)

    def finish_body(c, carry):
        finish(c, M_XCHUNK)
        return carry

    lax.fori_loop(0, nx, finish_body, 0)
    finish(nx, TAIL)


def _mlstm(proj, grow, gcol, g_head, seq, name):
    b, lp, _ = proj.shape
    nblk = seq // M_XCHUNK + 1
    return pl.pallas_call(
        functools.partial(_mlstm_kernel, seq=seq),
        grid=(b, M_HEADS),
        in_specs=[
            pl.BlockSpec((1, lp, M_DK), lambda i, h: (i, 0, h)),
            pl.BlockSpec((1, lp, M_DK), lambda i, h: (i, 0, M_HEADS + h)),
            pl.BlockSpec((1, lp, M_DV), lambda i, h: (i, 0, M_HEADS + h)),
            pl.BlockSpec((1, lp, M_DV), lambda i, h: (i, 0, 2 * M_HEADS + h)),
            pl.BlockSpec((1, lp, LANES), lambda i, h: (i, 0, 0)),
            pl.BlockSpec((1, nblk, 8, M_XCHUNK), lambda i, h: (i, 0, 0, 0)),
            pl.BlockSpec((1, M_DV), lambda i, h: (0, 0)),
        ],
        out_specs=pl.BlockSpec((1, lp, M_DV), lambda i, h: (i, 0, h)),
        out_shape=jax.ShapeDtypeStruct((b, lp, M_V_W), BF16),
        scratch_shapes=[
            pltpu.VMEM((lp, M_DV), F32),
            pltpu.VMEM((lp, M_DV), F32),
            pltpu.VMEM((2, M_DK, M_DV + LANES), F32),
            pltpu.VMEM((2, 1, 1), F32),
        ],
        compiler_params=_params("arbitrary", "arbitrary"),
        name=name,
    )(proj, proj, proj, proj, grow, gcol, g_head)


ATTN_TK = 2048
ATTN_TQ = 1056


ATTN_MIN_LOG2_SUM = -60.0
ATTN_BOUND_SLACK = 1.01


def _attn_kernel(q_ref, k_ref, v_ref, lam_ref, gs_ref, out_ref, *, seq, tq, lambda_init):
    lp = seq + TAIL
    tk = min(ATTN_TK, seq)
    q = q_ref[0]
    lane = lax.broadcasted_iota(jnp.int32, (tq, LANES), 1)
    zero = jnp.zeros_like(q)
    qs = (jnp.where(lane < D_HD, q, zero), jnp.where(lane >= D_HD, q, zero))
    kt = k_ref[0, seq:lp, :]
    vt = v_ref[0, seq:lp, :]
    tail_ok = lax.broadcasted_iota(jnp.int32, (tq, TAIL), 1) >= M_PAD
    lam_p = lam_ref[...]
    lam = (jnp.exp(jnp.sum(lam_p[0:1] * lam_p[1:2], axis=1, keepdims=True))
           - jnp.exp(jnp.sum(lam_p[2:3] * lam_p[3:4], axis=1, keepdims=True)) + lambda_init)

    def emit(acc1, l1, acc2, l2):
        o = acc1 / l1 - lam * (acc2 / l2)
        out_ref[0] = (_rms_rows(o, gs_ref[...]) * (1.0 - lambda_init)).astype(BF16)

    group = _group_ones(LANES, D_HD)
    kf = k_ref[0].astype(F32)
    ksq = jnp.max(jnp.dot((kf * kf).astype(BF16), group, preferred_element_type=F32),
                  axis=0, keepdims=True)
    qf = q.astype(F32)
    qsq = jnp.dot((qf * qf).astype(BF16), group, preferred_element_type=F32)
    bound = jnp.sqrt(qsq * ksq) * ATTN_BOUND_SLACK
    shifts = (jnp.max(jnp.where(lane < D_HD, bound, 0.0), axis=1, keepdims=True),
              jnp.max(jnp.where(lane >= D_HD, bound, 0.0), axis=1, keepdims=True))

    fast = []
    for qc, shift in zip(qs, shifts):
        s = jnp.where(tail_ok, lax.dot_general(qc, kt, _NT, preferred_element_type=F32), NEG)
        e = jnp.exp2(s - shift)
        fast += [jnp.sum(e, axis=1, keepdims=True), jnp.dot(e.astype(BF16), vt, preferred_element_type=F32)]

    def fast_body(j, st):
        rows = pl.ds(pl.multiple_of(j * tk, tk), tk)
        kc = k_ref[0, rows, :]
        vc = v_ref[0, rows, :]
        new = []
        for c, (qc, shift) in enumerate(zip(qs, shifts)):
            l, acc = st[2 * c:2 * c + 2]
            e = jnp.exp2(lax.dot_general(qc, kc, _NT, preferred_element_type=F32) - shift)
            new += [l + jnp.sum(e, axis=1, keepdims=True),
                    acc + jnp.dot(e.astype(BF16), vc, preferred_element_type=F32)]
        return tuple(new)

    l1, acc1, l2, acc2 = lax.fori_loop(0, seq // tk, fast_body, tuple(fast))
    emit(acc1, l1, acc2, l2)

    @pl.when(jnp.logical_not(jnp.min(jnp.minimum(l1, l2)) >= 2.0 ** ATTN_MIN_LOG2_SUM))
    def _():
        state = []
        for qc in qs:
            s = jnp.where(tail_ok, lax.dot_general(qc, kt, _NT, preferred_element_type=F32), NEG)
            m = jnp.max(s, axis=1, keepdims=True)
            e = jnp.exp2(s - m)
            state += [m, jnp.sum(e, axis=1, keepdims=True),
                      jnp.dot(e.astype(BF16), vt, preferred_element_type=F32)]

        def body(j, st):
            rows = pl.ds(pl.multiple_of(j * tk, tk), tk)
            kc = k_ref[0, rows, :]
            vc = v_ref[0, rows, :]
            new = []
            for c, qc in enumerate(qs):
                m, l, acc = st[3 * c:3 * c + 3]
                s = lax.dot_general(qc, kc, _NT, preferred_element_type=F32)
                m_new = jnp.maximum(m, jnp.max(s, axis=1, keepdims=True))
                a = jnp.exp2(m - m_new)
                e = jnp.exp2(s - m_new)
                new += [m_new, a * l + jnp.sum(e, axis=1, keepdims=True),
                        a * acc + jnp.dot(e.astype(BF16), vc, preferred_element_type=F32)]
            return tuple(new)

        _, x1, a1, _, x2, a2 = lax.fori_loop(0, seq // tk, body, tuple(state))
        emit(a1, x1, a2, x2)


def _attn(proj, lam, gs, seq, lambda_init, name):
    b, lp, _ = proj.shape
    tq = _row_tile(lp, ATTN_TQ)
    full = lambda i, h, j: (0, 0)
    return pl.pallas_call(
        functools.partial(_attn_kernel, seq=seq, tq=tq, lambda_init=lambda_init),
        grid=(b, D_HEADS, lp // tq),
        in_specs=[
            pl.BlockSpec((1, tq, LANES), lambda i, h, j: (i, j, h)),
            pl.BlockSpec((1, lp, LANES), lambda i, h, j: (i, 0, D_HEADS + h)),
            pl.BlockSpec((1, lp, LANES), lambda i, h, j: (i, 0, 2 * D_HEADS + h)),
            pl.BlockSpec((4, D_HD), full),
            pl.BlockSpec((1, LANES), full),
        ],
        out_specs=pl.BlockSpec((1, tq, LANES), lambda i, h, j: (i, j, h)),
        out_shape=jax.ShapeDtypeStruct((b, lp, D_V_W), BF16),
        compiler_params=_params("arbitrary", "arbitrary", "arbitrary"),
        name=name,
    )(proj, proj, proj, lam, gs)


def _rope_tables(seq, g_q, g_k):
    lp = seq + TAIL
    r = jnp.arange(lp)
    pos = jnp.where(r < seq, r + N_META, jnp.maximum(r - seq - M_PAD, 0)).astype(F32)
    half = ROPE_DIM // 2
    inv = ROPE_THETA ** (-jnp.arange(0, ROPE_DIM, 2, dtype=F32) / ROPE_DIM)
    ang = pos[:, None] * inv[None, :]
    cos, sin = jnp.cos(ang), jnp.sin(ang)
    rest = D_HD - ROPE_DIM

    def fold(g, scale):
        g = g.astype(F32) * scale
        cos_t = jnp.concatenate([cos * g[:half], cos * g[half:ROPE_DIM],
                                 jnp.broadcast_to(g[ROPE_DIM:], (lp, rest))], axis=1)
        sin_dn = jnp.concatenate([-sin * g[half:ROPE_DIM], jnp.zeros((lp, half + rest), F32)], axis=1)
        sin_up = jnp.concatenate([jnp.zeros((lp, half), F32), sin * g[:half],
                                  jnp.zeros((lp, rest), F32)], axis=1)
        return jnp.stack([jnp.tile(t, (1, 2)) for t in (cos_t, sin_dn, sin_up)])

    return jnp.stack([fold(g_q, D_HD ** -0.5 * math.log2(math.e)), fold(g_k, 1.0)])


def _trunk(x, meta_tokens, norm_mix, norm_ffn, m_w_main, m_w_gate, m_b_gate, m_g_head, m_w_out,
           d_w_in, d_g_q, d_g_k, d_lam, d_g_sub, d_w_out, ffn_w_in, ffn_w_out, tag):
    b, seq, d = x.shape
    lp = seq + TAIL
    rows = b * lp
    meta = jnp.broadcast_to(meta_tokens[None].astype(x.dtype), (b, N_META, d))
    h = jnp.concatenate([x, jnp.zeros((b, M_PAD, d), x.dtype), meta], axis=1)
    for i in range(DEPTH):
        j = i // 2
        gmix = norm_mix[i][None]
        hrows = h.reshape(rows, d)
        if i % 2 == 0:
            proj, gates = _m_in(hrows, gmix, m_w_main[j], m_w_gate[j], m_b_gate[j], f"m_in_{tag}{i}")
            grow, gcol = _gate_prep(gates.reshape(b, lp, LANES), seq, f"m_prep_{tag}{i}")
            mixed = _mlstm(proj.reshape(b, lp, M_MAIN_W), grow, gcol, m_g_head[j][None], seq,
                           f"m_cell_{tag}{i}")
            w_mix_out = m_w_out[j]
        else:
            lambda_init = 0.8 - 0.6 * math.exp(-0.3 * i)
            tables = _rope_tables(seq, d_g_q[j], d_g_k[j])
            proj = _qkv_rope(hrows, gmix, d_w_in[j], tables, lp, f"d_in_{tag}{i}")
            mixed = _attn(proj.reshape(b, lp, 3 * D_QK_W), d_lam[j], d_g_sub[j], seq, lambda_init,
                          f"d_attn_{tag}{i}")
            w_mix_out = d_w_out[j]
        h = _ffn(mixed, w_mix_out, h, norm_ffn[i][None], ffn_w_in[i], ffn_w_out[i], f"ffn_{tag}{i}",
                 out_seq=seq if i == DEPTH - 1 else None)
    return h


def kernel(x_prompt, x_sample, meta_tokens, norm_mix, norm_ffn, m_w_in, m_b_gate, m_g_head, m_w_out,
           d_w_in, d_g_q, d_g_k, d_lam_q1, d_lam_k1, d_lam_q2, d_lam_k2, d_g_sub, d_w_out,
           ffn_w_in, ffn_w_out):
    n_gate = 4 * M_HEADS
    m_w_main = m_w_in[:, :, :M_MAIN_W].astype(BF16)
    m_w_gate = jnp.pad(m_w_in[:, :, M_MAIN_W:], ((0, 0), (0, 0), (0, LANES - n_gate))).astype(BF16)
    b_gate = jnp.pad(m_b_gate.astype(F32), ((0, 0), (0, LANES - n_gate)))[:, None, :]
    d_lam = jnp.stack([d_lam_q1, d_lam_k1, d_lam_q2, d_lam_k2], axis=1).astype(F32)
    args = (meta_tokens, norm_mix.astype(F32), norm_ffn.astype(F32), m_w_main, m_w_gate, b_gate,
            m_g_head.astype(F32), m_w_out.astype(BF16), d_w_in.astype(BF16), d_g_q, d_g_k,
            d_lam, d_g_sub.astype(F32)[:, None, :], d_w_out.astype(BF16), ffn_w_in.astype(BF16),
            ffn_w_out.astype(BF16))
    return (_trunk(x_prompt, *args, "p"), _trunk(x_sample, *args, "s"))
```

```python
import functools
import math

import jax
import jax.numpy as jnp
from jax import lax
from jax.experimental import pallas as pl
from jax.experimental.pallas import tpu as pltpu

F32 = jnp.float32
BF16 = jnp.bfloat16

D_MODEL = 1024
DEPTH = 4
N_META = 16
RMS_EPS = 1e-6
NEG = -1e30
M_HEADS = 4
M_DK = 128
M_DV = 256
M_CHUNK = 64
M_XCHUNK = 256
M_PAD = M_CHUNK - N_META
M_QK_W = M_HEADS * M_DK
M_V_W = M_HEADS * M_DV
M_MAIN_W = 2 * M_QK_W + 2 * M_V_W
D_HEADS = 8
D_HD = 64
D_QK_W = D_HEADS * 2 * D_HD
D_V_W = D_HEADS * 2 * D_HD
ROPE_THETA = 500000.0
ROPE_DIM = D_HD // 4
FF = 2816
TAIL = M_CHUNK
LANES = 128
VMEM_LIMIT = 56 * 1024 * 1024


def _params(*sem):
    return pltpu.CompilerParams(dimension_semantics=sem, vmem_limit_bytes=VMEM_LIMIT)


def _row_tile(rows, target):
    best = 16
    for t in range(16, target + 1, 16):
        if rows % t == 0:
            best = t
    return best


def _rms_rows(x, g):
    ms = jnp.mean(x * x, axis=-1, keepdims=True)
    return x * lax.rsqrt(ms + RMS_EPS) * g


MM_COLS = 1024


def _m_in_kernel(x_ref, g_ref, w_ref, wg_ref, bg_ref, q_ref, k_ref, v_ref, o_ref, og_ref):
    xn = _rms_rows(x_ref[...], g_ref[...]).astype(BF16)
    outs = ((q_ref, M_DK), (k_ref, M_DK), (v_ref, M_DV), (o_ref, M_DV))
    col = 0
    for ref, width in outs:
        total = M_HEADS * width
        cols = min(MM_COLS, total)
        for c0 in range(0, total, cols):
            acc = jnp.dot(xn, w_ref[:, col + c0:col + c0 + cols],
                          preferred_element_type=F32).astype(BF16)
            for t in range(cols // width):
                ref[c0 // width + t] = acc[:, t * width:(t + 1) * width]
        col += total
    og_ref[...] = jnp.dot(xn, wg_ref[...], preferred_element_type=F32) + bg_ref[...]


def _m_in(x, g, w, wg, bg, name):
    rows, d = x.shape
    tm = _row_tile(rows, 1056)
    const = lambda i: (0, 0)
    head_major = lambda width: pl.BlockSpec((M_HEADS, tm, width), lambda i: (0, i, 0))
    widths = (M_DK, M_DK, M_DV, M_DV)
    return pl.pallas_call(
        _m_in_kernel,
        grid=(rows // tm,),
        in_specs=[
            pl.BlockSpec((tm, d), lambda i: (i, 0)),
            pl.BlockSpec((1, d), const),
            pl.BlockSpec((d, M_MAIN_W), const, pipeline_mode=pl.Buffered(1)),
            pl.BlockSpec((d, LANES), const, pipeline_mode=pl.Buffered(1)),
            pl.BlockSpec((1, LANES), const),
        ],
        out_specs=[head_major(wd) for wd in widths] + [pl.BlockSpec((tm, LANES), lambda i: (i, 0))],
        out_shape=[jax.ShapeDtypeStruct((M_HEADS, rows, wd), BF16) for wd in widths]
        + [jax.ShapeDtypeStruct((rows, LANES), F32)],
        compiler_params=_params("arbitrary"),
        name=name,
    )(x, g, w, wg, bg)


def _group_ones(width, group):
    gi = lax.broadcasted_iota(jnp.int32, (width, width), 0) // group
    gj = lax.broadcasted_iota(jnp.int32, (width, width), 1) // group
    return jnp.where(gi == gj, 1.0, 0.0).astype(BF16)


def _qkv_rope_kernel(x_ref, g_ref, w_ref, tab_ref, o_ref):
    xn = _rms_rows(x_ref[...], g_ref[...]).astype(BF16)
    wide = 2 * LANES
    group = _group_ones(wide, D_HD)
    half = ROPE_DIM // 2
    for j in range(3):
        acc = jnp.dot(xn, w_ref[:, j * D_QK_W:(j + 1) * D_QK_W], preferred_element_type=F32)
        if j == 2:
            o_ref[:, j * D_QK_W:(j + 1) * D_QK_W] = acc.astype(BF16)
            continue
        for p in range(D_QK_W // wide):
            x2 = acc[:, p * wide:(p + 1) * wide]
            ssq = jnp.dot((x2 * x2).astype(BF16), group, preferred_element_type=F32)
            xs = x2 * lax.rsqrt(ssq * (1.0 / D_HD) + RMS_EPS)
            for hf in range(2):
                xh = xs[:, hf * LANES:(hf + 1) * LANES]
                out = (xh * tab_ref[j, 0] + pltpu.roll(xh, LANES - half, 1) * tab_ref[j, 1]
                       + pltpu.roll(xh, half, 1) * tab_ref[j, 2])
                c0 = j * D_QK_W + p * wide + hf * LANES
                o_ref[:, c0:c0 + LANES] = out.astype(BF16)


def _qkv_rope(x, g, w, tables, lp, name):
    rows, d = x.shape
    n = w.shape[1]
    tm = _row_tile(lp, 1056)
    nb = lp // tm
    return pl.pallas_call(
        _qkv_rope_kernel,
        grid=(rows // tm,),
        in_specs=[
            pl.BlockSpec((tm, d), lambda i: (i, 0)),
            pl.BlockSpec((1, d), lambda i: (0, 0)),
            pl.BlockSpec((d, n), lambda i: (0, 0), pipeline_mode=pl.Buffered(1)),
            pl.BlockSpec((2, 3, tm, LANES), lambda i: (0, 0, i % nb, 0)),
        ],
        out_specs=pl.BlockSpec((tm, n), lambda i: (i, 0)),
        out_shape=jax.ShapeDtypeStruct((rows, n), BF16),
        compiler_params=_params("arbitrary"),
        name=name,
    )(x, g, w, tables)


FF_CHUNK = 512
FFN_ROWS = 512


def _ffn_kernel(a_ref, wm_ref, h_ref, g_ref, wi_ref, wo_ref, o_ref, xn_ref, acc_ref):
    if len(a_ref.shape) == 3:
        a = jnp.concatenate([a_ref[hd] for hd in range(a_ref.shape[0])], axis=1)
    else:
        a = a_ref[...]
    x = h_ref[...] + jnp.dot(a, wm_ref[...], preferred_element_type=F32)
    xn_ref[...] = _rms_rows(x, g_ref[...]).astype(BF16)
    acc_ref[...] = x
    for c0 in range(0, FF, FF_CHUNK):
        c1 = min(c0 + FF_CHUNK, FF)
        xn = xn_ref[...]
        gate = jnp.dot(xn, wi_ref[:, c0:c1], preferred_element_type=F32)
        up = jnp.dot(xn, wi_ref[:, FF + c0:FF + c1], preferred_element_type=F32)
        act = (gate * (1.0 / (1.0 + jnp.exp(-gate))) * up).astype(BF16)
        acc_ref[...] += jnp.dot(act, wo_ref[c0:c1, :], preferred_element_type=F32)
    o_ref[...] = acc_ref[...]


def _ffn(a, wm, h, g, wi, wo, name, out_seq=None):
    b, lp, d = h.shape
    const = lambda *_: (0, 0)
    weights = [
        pl.BlockSpec((d, d), const, pipeline_mode=pl.Buffered(1)),
        pl.BlockSpec((1, d), const),
        pl.BlockSpec((d, 2 * FF), const, pipeline_mode=pl.Buffered(1)),
        pl.BlockSpec((FF, d), const, pipeline_mode=pl.Buffered(1)),
    ]
    if out_seq is None:
        rows = b * lp
        tm = _row_tile(rows, FFN_ROWS)
        grid = (rows // tm,)
        tile = pl.BlockSpec((tm, d), lambda i: (i, 0))
        h = h.reshape(rows, d)
        if a.ndim == 4:
            a = a.reshape(a.shape[0], rows, a.shape[3])
            a_tile = pl.BlockSpec((a.shape[0], tm, a.shape[2]), lambda i: (0, i, 0))
        else:
            a, a_tile = a.reshape(rows, d), tile
        out_shape = jax.ShapeDtypeStruct((rows, d), F32)
        sem = ("arbitrary",)
    else:
        tm = _row_tile(out_seq, FFN_ROWS)
        grid = (b, out_seq // tm)
        tile = a_tile = pl.BlockSpec((None, tm, d), lambda i, j: (i, j, 0))
        if a.ndim == 4:
            a_tile = pl.BlockSpec((a.shape[0], None, tm, a.shape[3]), lambda i, j: (0, i, j, 0))
        out_shape = jax.ShapeDtypeStruct((b, out_seq, d), F32)
        sem = ("arbitrary", "arbitrary")
    out = pl.pallas_call(
        _ffn_kernel,
        grid=grid,
        in_specs=[a_tile, weights[0], tile, weights[1], weights[2], weights[3]],
        out_specs=tile,
        out_shape=out_shape,
        scratch_shapes=[pltpu.VMEM((tm, d), BF16), pltpu.VMEM((tm, d), F32)],
        compiler_params=_params(*sem),
        name=name,
    )(a, wm, h, g, wi, wo)
    return out if out_seq is not None else out.reshape(b, lp, d)


def _split3(x):
    hi = x.astype(BF16)
    r1 = x - hi.astype(F32)
    mid = r1.astype(BF16)
    lo = (r1 - mid.astype(F32)).astype(BF16)
    return hi, mid, lo


def _dot_exact01(m01, x, dims):
    out = None
    for part in _split3(x):
        t = lax.dot_general(m01, part, dims, preferred_element_type=F32)
        out = t if out is None else out + t
    return out


_NN = (((1,), (0,)), ((), ()))
_NT = (((1,), (1,)), ((), ()))
_TN = (((0,), (0,)), ((), ()))


def _gate_prep_kernel(g_ref, row_ref, col_ref, *, seq):
    rb = M_XCHUNK
    r = pl.program_id(1) * rb + lax.broadcasted_iota(jnp.int32, (rb, LANES), 0)
    valid = jnp.logical_or(r < seq, jnp.logical_and(r >= seq + M_PAD, r < seq + TAIL))
    g = jnp.where(valid, g_ref[0], 0.0)
    logf = jnp.where(valid, jnp.minimum(g, 0.0) - jnp.log(1.0 + jnp.exp(-jnp.abs(g))), 0.0)
    ig = jnp.where(valid, g, NEG)
    ti = lax.broadcasted_iota(jnp.int32, (rb, rb), 0)
    si = lax.broadcasted_iota(jnp.int32, (rb, rb), 1)
    pre_m = jnp.where(si <= ti, 1.0, 0.0).astype(BF16)
    pre = _dot_exact01(pre_m, logf, _NN)
    suf = pre[rb - 1:rb, :] - pre + logf
    lane = lax.broadcasted_iota(jnp.int32, (rb, LANES), 1)
    pre_dn = pltpu.roll(pre, LANES - M_HEADS, 1)
    suf_dn = pltpu.roll(suf, LANES - M_HEADS, 1)
    row = jnp.where(lane < 4, ig - pre_dn,
                    jnp.where(lane < 8, pre,
                              jnp.where(lane < 12, ig - suf_dn, suf)))
    row = jnp.where(lane < 16, row, 0.0)
    row_ref[0] = row
    er = lax.broadcasted_iota(jnp.int32, (8, LANES), 0)
    el = lax.broadcasted_iota(jnp.int32, (8, LANES), 1)
    pick = jnp.where(el == jnp.where(er < 4, er, er + 4), 1.0, 0.0).astype(BF16)
    col_ref[0, 0] = _dot_exact01(pick, row, _NT)


def _gate_prep(g, seq, name):
    b, lp, _ = g.shape
    nblk = seq // M_XCHUNK + 1
    return pl.pallas_call(
        functools.partial(_gate_prep_kernel, seq=seq),
        grid=(b, nblk),
        in_specs=[pl.BlockSpec((1, M_XCHUNK, LANES), lambda i, j: (i, j, 0))],
        out_specs=[
            pl.BlockSpec((1, M_XCHUNK, LANES), lambda i, j: (i, j, 0)),
            pl.BlockSpec((1, 1, 8, M_XCHUNK), lambda i, j: (i, j, 0, 0)),
        ],
        out_shape=[
            jax.ShapeDtypeStruct((b, lp, LANES), F32),
            jax.ShapeDtypeStruct((b, nblk, 8, M_XCHUNK), F32),
        ],
        compiler_params=_params("arbitrary", "arbitrary"),
        name=name,
    )(g)


def _lane_pick(x, idx):
    lane = lax.broadcasted_iota(jnp.int32, x.shape, 1)
    return jnp.sum(jnp.where(lane == idx, x, 0.0), axis=1, keepdims=True)


def _chunk_start(c):
    return c * M_XCHUNK if isinstance(c, int) else pl.multiple_of(c * M_XCHUNK, M_XCHUNK)


def _mlstm_kernel(q_ref, k_ref, v_ref, o_ref, gr_ref, gc_ref, gh_ref, out_ref,
                  hf_ref, hb_ref, cn_ref, m_ref, *, seq):
    head = pl.program_id(1)
    nx = seq // M_XCHUNK
    qk_scale = M_DK ** -0.5
    cn_ref[...] = jnp.zeros_like(cn_ref)
    m_ref[...] = jnp.zeros_like(m_ref)

    def step(d, c, size):
        rows = pl.ds(_chunk_start(c), size)
        ti = lax.broadcasted_iota(jnp.int32, (size, size), 0)
        si = lax.broadcasted_iota(jnp.int32, (size, size), 1)
        q = q_ref[rows, :]
        k = k_ref[rows, :]
        v1 = jnp.concatenate([v_ref[rows, :], jnp.ones((size, LANES), BF16)], axis=1)
        beta_r = _lane_pick(gr_ref[0, rows, :], 8 * d + 4 + head)
        alpha_c = gc_ref[0, c, pl.ds(4 * d + head, 1), :][:, :size]
        m_prev = m_ref[d]
        cn = cn_ref[d]
        keep = (si <= ti) if d == 0 else (si >= ti)
        dmat = jnp.where(keep, beta_r + alpha_c, -jnp.inf)
        m_inter = beta_r + m_prev
        m_comb = jnp.maximum(jnp.max(dmat, axis=1, keepdims=True), m_inter)
        s = lax.dot_general(q, k, _NT, preferred_element_type=F32)
        w = jnp.exp(dmat - m_comb) * s
        scale = jnp.exp(m_inter - m_comb)
        both = (jnp.dot(w.astype(BF16), v1, preferred_element_type=F32)
                + scale * jnp.dot(q, cn.astype(BF16), preferred_element_type=F32)) * qk_scale
        inv = 1.0 / jnp.maximum(jnp.abs(both[:, M_DV:]), jnp.exp(-m_comb))
        hval = both[:, :M_DV] * jnp.concatenate([inv, inv], axis=1)
        if d == 0:
            hf_ref[rows, :] = hval
            tot = beta_r[size - 1:size]
        else:
            hb_ref[rows, :] = hval
            tot = beta_r[0:1]
        a_max = jnp.max(alpha_c, axis=1, keepdims=True)
        m_new = jnp.maximum(tot + m_prev, tot + a_max)
        decay = jnp.exp(tot + m_prev - m_new)
        wk = jnp.exp(alpha_c + tot - m_new)
        kw = (k.T.astype(F32) * wk).astype(BF16)
        cn_ref[d] = decay * cn + jnp.dot(kw, v1, preferred_element_type=F32)
        m_ref[d] = m_new

    step(0, nx, TAIL)
    step(1, nx - 1, M_XCHUNK)

    def body(i, carry):
        step(0, i - 1, M_XCHUNK)
        step(1, nx - 1 - i, M_XCHUNK)
        return carry

    lax.fori_loop(1, nx, body, 0)
    step(0, nx - 1, M_XCHUNK)
    step(1, nx, TAIL)

    def finish(c, size):
        rows = pl.ds(_chunk_start(c), size)
        hsum = hf_ref[rows, :] + hb_ref[rows, :]
        og = o_ref[rows, :].astype(F32)
        out_ref[rows, :] = (_rms_rows(hsum, gh_ref[...]) * (1.0 / (1.0 + jnp.exp(-og)))).astype(BF16)

    def finish_body(c, carry):
        finish(c, M_XCHUNK)
        return carry

    lax.fori_loop(0, nx, finish_body, 0)
    finish(nx, TAIL)


def _mlstm(q, k, v, o, grow, gcol, g_head, seq, name):
    _, b, lp, _ = q.shape
    nblk = seq // M_XCHUNK + 1
    per_head = lambda width: pl.BlockSpec((None, None, lp, width), lambda i, h: (h, i, 0, 0))
    return pl.pallas_call(
        functools.partial(_mlstm_kernel, seq=seq),
        grid=(b, M_HEADS),
        in_specs=[
            per_head(M_DK),
            per_head(M_DK),
            per_head(M_DV),
            per_head(M_DV),
            pl.BlockSpec((1, lp, LANES), lambda i, h: (i, 0, 0)),
            pl.BlockSpec((1, nblk, 8, M_XCHUNK), lambda i, h: (i, 0, 0, 0)),
            pl.BlockSpec((1, M_DV), lambda i, h: (0, 0)),
        ],
        out_specs=per_head(M_DV),
        out_shape=jax.ShapeDtypeStruct((M_HEADS, b, lp, M_DV), BF16),
        scratch_shapes=[
            pltpu.VMEM((lp, M_DV), F32),
            pltpu.VMEM((lp, M_DV), F32),
            pltpu.VMEM((2, M_DK, M_DV + LANES), F32),
            pltpu.VMEM((2, 1, 1), F32),
        ],
        compiler_params=_params("arbitrary", "arbitrary"),
        name=name,
    )(q, k, v, o, grow, gcol, g_head)


ATTN_TK = 2048
ATTN_TQ = 1056


ATTN_MIN_LOG2_SUM = -60.0
ATTN_BOUND_SLACK = 1.02


def _attn_kernel(q_ref, k_ref, v_ref, lam_ref, gk_ref, gs_ref, out_ref, *, seq, tq, lambda_init):
    lp = seq + TAIL
    tk = min(ATTN_TK, seq)
    q = q_ref[0]
    lane = lax.broadcasted_iota(jnp.int32, (tq, LANES), 1)
    zero = jnp.zeros_like(q)
    qs = (jnp.where(lane < D_HD, q, zero), jnp.where(lane >= D_HD, q, zero))
    kt = k_ref[0, seq:lp, :]
    vt = v_ref[0, seq:lp, :]
    tail_ok = lax.broadcasted_iota(jnp.int32, (tq, TAIL), 1) >= M_PAD
    lam_p = lam_ref[...]
    lam = (jnp.exp(jnp.sum(lam_p[0:1] * lam_p[1:2], axis=1, keepdims=True))
           - jnp.exp(jnp.sum(lam_p[2:3] * lam_p[3:4], axis=1, keepdims=True)) + lambda_init)

    def emit(acc1, l1, acc2, l2):
        o = acc1 / l1 - lam * (acc2 / l2)
        out_ref[0] = (_rms_rows(o, gs_ref[...]) * (1.0 - lambda_init)).astype(BF16)

    group = _group_ones(LANES, D_HD)
    ksq = D_HD * jnp.max(gk_ref[...] * gk_ref[...], axis=1, keepdims=True)
    qf = q.astype(F32)
    qsq = jnp.dot((qf * qf).astype(BF16), group, preferred_element_type=F32)
    bound = jnp.sqrt(qsq * ksq) * ATTN_BOUND_SLACK
    shifts = (jnp.max(jnp.where(lane < D_HD, bound, 0.0), axis=1, keepdims=True),
              jnp.max(jnp.where(lane >= D_HD, bound, 0.0), axis=1, keepdims=True))

    fast = []
    for qc, shift in zip(qs, shifts):
        s = jnp.where(tail_ok, lax.dot_general(qc, kt, _NT, preferred_element_type=F32), NEG)
        e = jnp.exp2(s - shift)
        fast += [jnp.sum(e, axis=1, keepdims=True), jnp.dot(e.astype(BF16), vt, preferred_element_type=F32)]

    def fast_body(j, st):
        rows = pl.ds(pl.multiple_of(j * tk, tk), tk)
        kc = k_ref[0, rows, :]
        vc = v_ref[0, rows, :]
        new = []
        for c, (qc, shift) in enumerate(zip(qs, shifts)):
            l, acc = st[2 * c:2 * c + 2]
            e = jnp.exp2(lax.dot_general(qc, kc, _NT, preferred_element_type=F32) - shift)
            new += [l + jnp.sum(e, axis=1, keepdims=True),
                    acc + jnp.dot(e.astype(BF16), vc, preferred_element_type=F32)]
        return tuple(new)

    l1, acc1, l2, acc2 = lax.fori_loop(0, seq // tk, fast_body, tuple(fast))
    emit(acc1, l1, acc2, l2)

    @pl.when(jnp.logical_not(jnp.min(jnp.minimum(l1, l2)) >= 2.0 ** ATTN_MIN_LOG2_SUM))
    def _():
        state = []
        for qc in qs:
            s = jnp.where(tail_ok, lax.dot_general(qc, kt, _NT, preferred_element_type=F32), NEG)
            m = jnp.max(s, axis=1, keepdims=True)
            e = jnp.exp2(s - m)
            state += [m, jnp.sum(e, axis=1, keepdims=True),
                      jnp.dot(e.astype(BF16), vt, preferred_element_type=F32)]

        def body(j, st):
            rows = pl.ds(pl.multiple_of(j * tk, tk), tk)
            kc = k_ref[0, rows, :]
            vc = v_ref[0, rows, :]
            new = []
            for c, qc in enumerate(qs):
                m, l, acc = st[3 * c:3 * c + 3]
                s = lax.dot_general(qc, kc, _NT, preferred_element_type=F32)
                m_new = jnp.maximum(m, jnp.max(s, axis=1, keepdims=True))
                a = jnp.exp2(m - m_new)
                e = jnp.exp2(s - m_new)
                new += [m_new, a * l + jnp.sum(e, axis=1, keepdims=True),
                        a * acc + jnp.dot(e.astype(BF16), vc, preferred_element_type=F32)]
            return tuple(new)

        _, x1, a1, _, x2, a2 = lax.fori_loop(0, seq // tk, body, tuple(state))
        emit(a1, x1, a2, x2)


def _attn(proj, lam, gk, gs, seq, lambda_init, name):
    b, lp, _ = proj.shape
    tq = _row_tile(lp, ATTN_TQ)
    full = lambda i, h, j: (0, 0)
    return pl.pallas_call(
        functools.partial(_attn_kernel, seq=seq, tq=tq, lambda_init=lambda_init),
        grid=(b, D_HEADS, lp // tq),
        in_specs=[
            pl.BlockSpec((1, tq, LANES), lambda i, h, j: (i, j, h)),
            pl.BlockSpec((1, lp, LANES), lambda i, h, j: (i, 0, D_HEADS + h)),
            pl.BlockSpec((1, lp, LANES), lambda i, h, j: (i, 0, 2 * D_HEADS + h)),
            pl.BlockSpec((4, D_HD), full),
            pl.BlockSpec((1, D_HD), full),
            pl.BlockSpec((1, LANES), full),
        ],
        out_specs=pl.BlockSpec((1, tq, LANES), lambda i, h, j: (i, j, h)),
        out_shape=jax.ShapeDtypeStruct((b, lp, D_V_W), BF16),
        compiler_params=_params("arbitrary", "arbitrary", "arbitrary"),
        name=name,
    )(proj, proj, proj, lam, gk, gs)


def _rope_tables(seq, g_q, g_k):
    lp = seq + TAIL
    r = jnp.arange(lp)
    pos = jnp.where(r < seq, r + N_META, jnp.maximum(r - seq - M_PAD, 0)).astype(F32)
    half = ROPE_DIM // 2
    inv = ROPE_THETA ** (-jnp.arange(0, ROPE_DIM, 2, dtype=F32) / ROPE_DIM)
    ang = pos[:, None] * inv[None, :]
    cos, sin = jnp.cos(ang), jnp.sin(ang)
    rest = D_HD - ROPE_DIM

    def fold(g, scale):
        g = g.astype(F32) * scale
        cos_t = jnp.concatenate([cos * g[:half], cos * g[half:ROPE_DIM],
                                 jnp.broadcast_to(g[ROPE_DIM:], (lp, rest))], axis=1)
        sin_dn = jnp.concatenate([-sin * g[half:ROPE_DIM], jnp.zeros((lp, half + rest), F32)], axis=1)
        sin_up = jnp.concatenate([jnp.zeros((lp, half), F32), sin * g[:half],
                                  jnp.zeros((lp, rest), F32)], axis=1)
        return jnp.stack([jnp.tile(t, (1, 2)) for t in (cos_t, sin_dn, sin_up)])

    return jnp.stack([fold(g_q, D_HD ** -0.5 * math.log2(math.e)), fold(g_k, 1.0)])


def _trunk(x, meta_tokens, norm_mix, norm_ffn, m_w_main, m_w_gate, m_b_gate, m_g_head, m_w_out,
           d_w_in, d_g_q, d_g_k, d_lam, d_g_sub, d_w_out, ffn_w_in, ffn_w_out, tag):
    b, seq, d = x.shape
    lp = seq + TAIL
    rows = b * lp
    meta = jnp.broadcast_to(meta_tokens[None].astype(x.dtype), (b, N_META, d))
    h = jnp.concatenate([x, jnp.zeros((b, M_PAD, d), x.dtype), meta], axis=1)
    for i in range(DEPTH):
        j = i // 2
        gmix = norm_mix[i][None]
        hrows = h.reshape(rows, d)
        if i % 2 == 0:
            *qkvo, gates = _m_in(hrows, gmix, m_w_main[j], m_w_gate[j], m_b_gate[j], f"m_in_{tag}{i}")
            grow, gcol = _gate_prep(gates.reshape(b, lp, LANES), seq, f"m_prep_{tag}{i}")
            qkvo = [t.reshape(M_HEADS, b, lp, t.shape[-1]) for t in qkvo]
            mixed = _mlstm(*qkvo, grow, gcol, m_g_head[j][None], seq, f"m_cell_{tag}{i}")
            w_mix_out = m_w_out[j]
        else:
            lambda_init = 0.8 - 0.6 * math.exp(-0.3 * i)
            tables = _rope_tables(seq, d_g_q[j], d_g_k[j])
            proj = _qkv_rope(hrows, gmix, d_w_in[j], tables, lp, f"d_in_{tag}{i}")
            mixed = _attn(proj.reshape(b, lp, 3 * D_QK_W), d_lam[j], d_g_k[j].astype(F32)[None],
                          d_g_sub[j], seq, lambda_init, f"d_attn_{tag}{i}")
            w_mix_out = d_w_out[j]
        h = _ffn(mixed, w_mix_out, h, norm_ffn[i][None], ffn_w_in[i], ffn_w_out[i], f"ffn_{tag}{i}",
                 out_seq=seq if i == DEPTH - 1 else None)
    return h


def kernel(x_prompt, x_sample, meta_tokens, norm_mix, norm_ffn, m_w_in, m_b_gate, m_g_head, m_w_out,
           d_w_in, d_g_q, d_g_k, d_lam_q1, d_lam_k1, d_lam_q2, d_lam_k2, d_g_sub, d_w_out,
           ffn_w_in, ffn_w_out):
    n_gate = 4 * M_HEADS
    m_w_main = m_w_in[:, :, :M_MAIN_W].astype(BF16)
    m_w_gate = jnp.pad(m_w_in[:, :, M_MAIN_W:], ((0, 0), (0, 0), (0, LANES - n_gate))).astype(BF16)
    b_gate = jnp.pad(m_b_gate.astype(F32), ((0, 0), (0, LANES - n_gate)))[:, None, :]
    d_lam = jnp.stack([d_lam_q1, d_lam_k1, d_lam_q2, d_lam_k2], axis=1).astype(F32)
    args = (meta_tokens, norm_mix.astype(F32), norm_ffn.astype(F32), m_w_main, m_w_gate, b_gate,
            m_g_head.astype(F32), m_w_out.astype(BF16), d_w_in.astype(BF16), d_g_q, d_g_k,
            d_lam, d_g_sub.astype(F32)[:, None, :], d_w_out.astype(BF16), ffn_w_in.astype(BF16),
            ffn_w_out.astype(BF16))
    return (_trunk(x_prompt, *args, "p"), _trunk(x_sample, *args, "s"))
```

```python
import functools
import math

import jax
import jax.numpy as jnp
from jax import lax
from jax.experimental import pallas as pl
from jax.experimental.pallas import tpu as pltpu

F32 = jnp.float32
BF16 = jnp.bfloat16

D_MODEL = 1024
DEPTH = 4
N_META = 16
RMS_EPS = 1e-6
NEG = -1e30
M_HEADS = 4
M_DK = 128
M_DV = 256
M_CHUNK = 64
M_XCHUNK = 256
M_PAD = M_CHUNK - N_META
M_QK_W = M_HEADS * M_DK
M_V_W = M_HEADS * M_DV
M_MAIN_W = 2 * M_QK_W + 2 * M_V_W
D_HEADS = 8
D_HD = 64
D_QK_W = D_HEADS * 2 * D_HD
D_V_W = D_HEADS * 2 * D_HD
ROPE_THETA = 500000.0
ROPE_DIM = D_HD // 4
FF = 2816
TAIL = M_CHUNK
LANES = 128
VMEM_LIMIT = 56 * 1024 * 1024


def _params(*sem):
    return pltpu.CompilerParams(dimension_semantics=sem, vmem_limit_bytes=VMEM_LIMIT)


def _row_tile(rows, target):
    best = 16
    for t in range(16, target + 1, 16):
        if rows % t == 0:
            best = t
    return best


def _rms_rows(x, g):
    ms = jnp.mean(x * x, axis=-1, keepdims=True)
    return x * lax.rsqrt(ms + RMS_EPS) * g


MM_COLS = 1024


def _m_in_kernel(x_ref, g_ref, w_ref, wg_ref, bg_ref, q_ref, k_ref, v_ref, o_ref, og_ref):
    xn = _rms_rows(x_ref[...], g_ref[...]).astype(BF16)
    outs = ((q_ref, M_DK), (k_ref, M_DK), (v_ref, M_DV), (o_ref, M_DV))
    col = 0
    for ref, width in outs:
        total = M_HEADS * width
        cols = min(MM_COLS, total)
        for c0 in range(0, total, cols):
            acc = jnp.dot(xn, w_ref[:, col + c0:col + c0 + cols],
                          preferred_element_type=F32).astype(BF16)
            for t in range(cols // width):
                ref[c0 // width + t] = acc[:, t * width:(t + 1) * width]
        col += total
    og_ref[...] = jnp.dot(xn, wg_ref[...], preferred_element_type=F32) + bg_ref[...]


def _m_in(x, g, w, wg, bg, name):
    rows, d = x.shape
    tm = _row_tile(rows, 1056)
    const = lambda i: (0, 0)
    head_major = lambda width: pl.BlockSpec((M_HEADS, tm, width), lambda i: (0, i, 0))
    widths = (M_DK, M_DK, M_DV, M_DV)
    return pl.pallas_call(
        _m_in_kernel,
        grid=(rows // tm,),
        in_specs=[
            pl.BlockSpec((tm, d), lambda i: (i, 0)),
            pl.BlockSpec((1, d), const),
            pl.BlockSpec((d, M_MAIN_W), const, pipeline_mode=pl.Buffered(1)),
            pl.BlockSpec((d, LANES), const, pipeline_mode=pl.Buffered(1)),
            pl.BlockSpec((1, LANES), const),
        ],
        out_specs=[head_major(wd) for wd in widths] + [pl.BlockSpec((tm, LANES), lambda i: (i, 0))],
        out_shape=[jax.ShapeDtypeStruct((M_HEADS, rows, wd), BF16) for wd in widths]
        + [jax.ShapeDtypeStruct((rows, LANES), F32)],
        compiler_params=_params("arbitrary"),
        name=name,
    )(x, g, w, wg, bg)


def _group_ones(width, group):
    gi = lax.broadcasted_iota(jnp.int32, (width, width), 0) // group
    gj = lax.broadcasted_iota(jnp.int32, (width, width), 1) // group
    return jnp.where(gi == gj, 1.0, 0.0).astype(BF16)


def _qkv_rope_kernel(x_ref, g_ref, w_ref, tab_ref, o_ref):
    xn = _rms_rows(x_ref[...], g_ref[...]).astype(BF16)
    wide = 2 * LANES
    group = _group_ones(wide, D_HD)
    half = ROPE_DIM // 2
    for j in range(3):
        acc = jnp.dot(xn, w_ref[:, j * D_QK_W:(j + 1) * D_QK_W], preferred_element_type=F32)
        if j == 2:
            o_ref[:, j * D_QK_W:(j + 1) * D_QK_W] = acc.astype(BF16)
            continue
        for p in range(D_QK_W // wide):
            x2 = acc[:, p * wide:(p + 1) * wide]
            ssq = jnp.dot((x2 * x2).astype(BF16), group, preferred_element_type=F32)
            xs = x2 * lax.rsqrt(ssq * (1.0 / D_HD) + RMS_EPS)
            for hf in range(2):
                xh = xs[:, hf * LANES:(hf + 1) * LANES]
                out = (xh * tab_ref[j, 0] + pltpu.roll(xh, LANES - half, 1) * tab_ref[j, 1]
                       + pltpu.roll(xh, half, 1) * tab_ref[j, 2])
                c0 = j * D_QK_W + p * wide + hf * LANES
                o_ref[:, c0:c0 + LANES] = out.astype(BF16)


def _qkv_rope(x, g, w, tables, lp, name):
    rows, d = x.shape
    n = w.shape[1]
    tm = _row_tile(lp, 1056)
    nb = lp // tm
    return pl.pallas_call(
        _qkv_rope_kernel,
        grid=(rows // tm,),
        in_specs=[
            pl.BlockSpec((tm, d), lambda i: (i, 0)),
            pl.BlockSpec((1, d), lambda i: (0, 0)),
            pl.BlockSpec((d, n), lambda i: (0, 0), pipeline_mode=pl.Buffered(1)),
            pl.BlockSpec((2, 3, tm, LANES), lambda i: (0, 0, i % nb, 0)),
        ],
        out_specs=pl.BlockSpec((tm, n), lambda i: (i, 0)),
        out_shape=jax.ShapeDtypeStruct((rows, n), BF16),
        compiler_params=_params("arbitrary"),
        name=name,
    )(x, g, w, tables)


FF_CHUNK = 512
FFN_ROWS = 512


def _ffn_kernel(a_ref, wm_ref, h_ref, g_ref, wi_ref, wo_ref, o_ref, xn_ref, acc_ref):
    if len(a_ref.shape) == 3:
        a = jnp.concatenate([a_ref[hd] for hd in range(a_ref.shape[0])], axis=1)
    else:
        a = a_ref[...]
    x = h_ref[...] + jnp.dot(a, wm_ref[...], preferred_element_type=F32)
    xn_ref[...] = _rms_rows(x, g_ref[...]).astype(BF16)
    acc_ref[...] = x
    for c0 in range(0, FF, FF_CHUNK):
        c1 = min(c0 + FF_CHUNK, FF)
        xn = xn_ref[...]
        gate = jnp.dot(xn, wi_ref[:, c0:c1], preferred_element_type=F32)
        up = jnp.dot(xn, wi_ref[:, FF + c0:FF + c1], preferred_element_type=F32)
        act = (gate * (1.0 / (1.0 + jnp.exp(-gate))) * up).astype(BF16)
        acc_ref[...] += jnp.dot(act, wo_ref[c0:c1, :], preferred_element_type=F32)
    o_ref[...] = acc_ref[...]


def _ffn(a, wm, h, g, wi, wo, name, out_seq=None):
    b, lp, d = h.shape
    const = lambda *_: (0, 0)
    weights = [
        pl.BlockSpec((d, d), const, pipeline_mode=pl.Buffered(1)),
        pl.BlockSpec((1, d), const),
        pl.BlockSpec((d, 2 * FF), const, pipeline_mode=pl.Buffered(1)),
        pl.BlockSpec((FF, d), const, pipeline_mode=pl.Buffered(1)),
    ]
    if out_seq is None:
        rows = b * lp
        tm = _row_tile(rows, FFN_ROWS)
        grid = (rows // tm,)
        tile = pl.BlockSpec((tm, d), lambda i: (i, 0))
        h = h.reshape(rows, d)
        if a.ndim == 4:
            a = a.reshape(a.shape[0], rows, a.shape[3])
            a_tile = pl.BlockSpec((a.shape[0], tm, a.shape[2]), lambda i: (0, i, 0))
        else:
            a, a_tile = a.reshape(rows, d), tile
        out_shape = jax.ShapeDtypeStruct((rows, d), F32)
        sem = ("arbitrary",)
    else:
        tm = _row_tile(out_seq, FFN_ROWS)
        grid = (b, out_seq // tm)
        tile = a_tile = pl.BlockSpec((None, tm, d), lambda i, j: (i, j, 0))
        if a.ndim == 4:
            a_tile = pl.BlockSpec((a.shape[0], None, tm, a.shape[3]), lambda i, j: (0, i, j, 0))
        out_shape = jax.ShapeDtypeStruct((b, out_seq, d), F32)
        sem = ("arbitrary", "arbitrary")
    out = pl.pallas_call(
        _ffn_kernel,
        grid=grid,
        in_specs=[a_tile, weights[0], tile, weights[1], weights[2], weights[3]],
        out_specs=tile,
        out_shape=out_shape,
        scratch_shapes=[pltpu.VMEM((tm, d), BF16), pltpu.VMEM((tm, d), F32)],
        compiler_params=_params(*sem),
        name=name,
    )(a, wm, h, g, wi, wo)
    return out if out_seq is not None else out.reshape(b, lp, d)


def _split3(x):
    hi = x.astype(BF16)
    r1 = x - hi.astype(F32)
    mid = r1.astype(BF16)
    lo = (r1 - mid.astype(F32)).astype(BF16)
    return hi, mid, lo


def _dot_exact01(m01, x, dims):
    out = None
    for part in _split3(x):
        t = lax.dot_general(m01, part, dims, preferred_element_type=F32)
        out = t if out is None else out + t
    return out


_NN = (((1,), (0,)), ((), ()))
_NT = (((1,), (1,)), ((), ()))
_TN = (((0,), (0,)), ((), ()))


def _gate_prep_kernel(g_ref, row_ref, col_ref, *, seq):
    rb = M_XCHUNK
    r = pl.program_id(1) * rb + lax.broadcasted_iota(jnp.int32, (rb, LANES), 0)
    valid = jnp.logical_or(r < seq, jnp.logical_and(r >= seq + M_PAD, r < seq + TAIL))
    g = jnp.where(valid, g_ref[0], 0.0)
    logf = jnp.where(valid, jnp.minimum(g, 0.0) - jnp.log(1.0 + jnp.exp(-jnp.abs(g))), 0.0)
    ig = jnp.where(valid, g, NEG)
    ti = lax.broadcasted_iota(jnp.int32, (rb, rb), 0)
    si = lax.broadcasted_iota(jnp.int32, (rb, rb), 1)
    pre_m = jnp.where(si <= ti, 1.0, 0.0).astype(BF16)
    pre = _dot_exact01(pre_m, logf, _NN)
    suf = pre[rb - 1:rb, :] - pre + logf
    lane = lax.broadcasted_iota(jnp.int32, (rb, LANES), 1)
    pre_dn = pltpu.roll(pre, LANES - M_HEADS, 1)
    suf_dn = pltpu.roll(suf, LANES - M_HEADS, 1)
    row = jnp.where(lane < 4, ig - pre_dn,
                    jnp.where(lane < 8, pre,
                              jnp.where(lane < 12, ig - suf_dn, suf)))
    row = jnp.where(lane < 16, row, 0.0)
    row_ref[0] = row
    er = lax.broadcasted_iota(jnp.int32, (8, LANES), 0)
    el = lax.broadcasted_iota(jnp.int32, (8, LANES), 1)
    pick = jnp.where(el == jnp.where(er < 4, er, er + 4), 1.0, 0.0).astype(BF16)
    col_ref[0, 0] = _dot_exact01(pick, row, _NT)


def _gate_prep(g, seq, name):
    b, lp, _ = g.shape
    nblk = seq // M_XCHUNK + 1
    return pl.pallas_call(
        functools.partial(_gate_prep_kernel, seq=seq),
        grid=(b, nblk),
        in_specs=[pl.BlockSpec((1, M_XCHUNK, LANES), lambda i, j: (i, j, 0))],
        out_specs=[
            pl.BlockSpec((1, M_XCHUNK, LANES), lambda i, j: (i, j, 0)),
            pl.BlockSpec((1, 1, 8, M_XCHUNK), lambda i, j: (i, j, 0, 0)),
        ],
        out_shape=[
            jax.ShapeDtypeStruct((b, lp, LANES), F32),
            jax.ShapeDtypeStruct((b, nblk, 8, M_XCHUNK), F32),
        ],
        compiler_params=_params("arbitrary", "arbitrary"),
        name=name,
    )(g)


def _lane_pick(x, idx):
    lane = lax.broadcasted_iota(jnp.int32, x.shape, 1)
    return jnp.sum(jnp.where(lane == idx, x, 0.0), axis=1, keepdims=True)


def _chunk_start(c):
    return c * M_XCHUNK if isinstance(c, int) else pl.multiple_of(c * M_XCHUNK, M_XCHUNK)


def _mlstm_kernel(q_ref, k_ref, v_ref, o_ref, gr_ref, gc_ref, gh_ref, out_ref,
                  hf_ref, hb_ref, cn_ref, m_ref, *, seq):
    head = pl.program_id(1)
    nx = seq // M_XCHUNK
    qk_scale = M_DK ** -0.5
    cn_ref[...] = jnp.zeros_like(cn_ref)
    m_ref[...] = jnp.zeros_like(m_ref)

    def step(d, c, size):
        def wide(x, n):
            return x[:, :n] if n < LANES else jnp.concatenate([x] * (n // LANES), axis=1)

        rows = pl.ds(_chunk_start(c), size)
        ti = lax.broadcasted_iota(jnp.int32, (size, size), 0)
        si = lax.broadcasted_iota(jnp.int32, (size, size), 1)
        q = q_ref[rows, :]
        k = k_ref[rows, :]
        v1 = jnp.concatenate([v_ref[rows, :], jnp.ones((size, LANES), BF16)], axis=1)
        beta = jnp.broadcast_to(_lane_pick(gr_ref[0, rows, :], 8 * d + 4 + head), (size, LANES))
        alpha_c = gc_ref[0, c, pl.ds(4 * d + head, 1), :][:, :size]
        m_prev = m_ref[d]
        cn = cn_ref[d]
        keep = (si <= ti) if d == 0 else (si >= ti)
        dmat = jnp.where(keep, wide(beta, size) + alpha_c, -jnp.inf)
        m_inter = beta + m_prev
        m_comb = jnp.maximum(jnp.broadcast_to(jnp.max(dmat, axis=1, keepdims=True), (size, LANES)),
                             m_inter)
        s = lax.dot_general(q, k, _NT, preferred_element_type=F32)
        w = jnp.exp(dmat - wide(m_comb, size)) * s
        scale = jnp.exp(m_inter - m_comb)
        both = (jnp.dot(w.astype(BF16), v1, preferred_element_type=F32)
                + wide(scale, M_DV + LANES)
                * jnp.dot(q, cn.astype(BF16), preferred_element_type=F32)) * qk_scale
        inv = 1.0 / jnp.maximum(jnp.abs(both[:, M_DV:]), jnp.exp(-m_comb))
        hval = both[:, :M_DV] * wide(inv, M_DV)
        if d == 0:
            hf_ref[rows, :] = hval
            tot = beta[size - 1:size]
        else:
            hb_ref[rows, :] = hval
            tot = beta[0:1]
        a_max = jnp.max(alpha_c, axis=1, keepdims=True)
        m_new = jnp.maximum(tot + m_prev, tot + a_max)
        decay = jnp.exp(tot + m_prev - m_new)
        wk = jnp.exp(alpha_c + wide(tot - m_new, size))
        kw = (k.T.astype(F32) * wk).astype(BF16)
        cn_ref[d] = wide(decay, M_DV + LANES) * cn + jnp.dot(kw, v1, preferred_element_type=F32)
        m_ref[d] = m_new

    step(0, nx, TAIL)
    step(1, nx - 1, M_XCHUNK)

    def body(i, carry):
        step(0, i - 1, M_XCHUNK)
        step(1, nx - 1 - i, M_XCHUNK)
        return carry

    lax.fori_loop(1, nx, body, 0, unroll=2)
    step(0, nx - 1, M_XCHUNK)
    step(1, nx, TAIL)

    def finish(c, size):
        rows = pl.ds(_chunk_start(c), size)
        hsum = hf_ref[rows, :] + hb_ref[rows, :]
        og = o_ref[rows, :].astype(F32)
        out_ref[rows, :] = (_rms_rows(hsum, gh_ref[...]) * (1.0 / (1.0 + jnp.exp(-og)))).astype(BF16)

    def finish_body(c, carry):
        finish(c, M_XCHUNK)
        return carry

    lax.fori_loop(0, nx, finish_body, 0)
    finish(nx, TAIL)


def _mlstm(q, k, v, o, grow, gcol, g_head, seq, name):
    _, b, lp, _ = q.shape
    nblk = seq // M_XCHUNK + 1
    per_head = lambda width: pl.BlockSpec((None, None, lp, width), lambda i, h: (h, i, 0, 0))
    return pl.pallas_call(
        functools.partial(_mlstm_kernel, seq=seq),
        grid=(b, M_HEADS),
        in_specs=[
            per_head(M_DK),
            per_head(M_DK),
            per_head(M_DV),
            per_head(M_DV),
            pl.BlockSpec((1, lp, LANES), lambda i, h: (i, 0, 0)),
            pl.BlockSpec((1, nblk, 8, M_XCHUNK), lambda i, h: (i, 0, 0, 0)),
            pl.BlockSpec((1, M_DV), lambda i, h: (0, 0)),
        ],
        out_specs=per_head(M_DV),
        out_shape=jax.ShapeDtypeStruct((M_HEADS, b, lp, M_DV), BF16),
        scratch_shapes=[
            pltpu.VMEM((lp, M_DV), F32),
            pltpu.VMEM((lp, M_DV), F32),
            pltpu.VMEM((2, M_DK, M_DV + LANES), F32),
            pltpu.VMEM((2, 1, LANES), F32),
        ],
        compiler_params=_params("arbitrary", "arbitrary"),
        name=name,
    )(q, k, v, o, grow, gcol, g_head)


ATTN_TK = 2048
ATTN_TQ = 1056


ATTN_MIN_LOG2_SUM = -60.0
ATTN_BOUND_SLACK = 1.02


def _attn_kernel(q_ref, k_ref, v_ref, lam_ref, gk_ref, gs_ref, out_ref, *, seq, tq, lambda_init):
    lp = seq + TAIL
    tk = min(ATTN_TK, seq)
    q = q_ref[0]
    lane = lax.broadcasted_iota(jnp.int32, (tq, LANES), 1)
    zero = jnp.zeros_like(q)
    qs = (jnp.where(lane < D_HD, q, zero), jnp.where(lane >= D_HD, q, zero))
    kt = k_ref[0, seq:lp, :]
    vt = v_ref[0, seq:lp, :]
    tail_ok = lax.broadcasted_iota(jnp.int32, (tq, TAIL), 1) >= M_PAD
    lam_p = lam_ref[...]
    lam = (jnp.exp(jnp.sum(lam_p[0:1] * lam_p[1:2], axis=1, keepdims=True))
           - jnp.exp(jnp.sum(lam_p[2:3] * lam_p[3:4], axis=1, keepdims=True)) + lambda_init)

    def emit(acc1, l1, acc2, l2):
        o = acc1 / l1 - lam * (acc2 / l2)
        out_ref[0] = (_rms_rows(o, gs_ref[...]) * (1.0 - lambda_init)).astype(BF16)

    group = _group_ones(LANES, D_HD)
    ksq = D_HD * jnp.max(gk_ref[...] * gk_ref[...], axis=1, keepdims=True)
    qf = q.astype(F32)
    qsq = jnp.dot((qf * qf).astype(BF16), group, preferred_element_type=F32)
    bound = jnp.sqrt(qsq * ksq) * ATTN_BOUND_SLACK
    shifts = (jnp.max(jnp.where(lane < D_HD, bound, 0.0), axis=1, keepdims=True),
              jnp.max(jnp.where(lane >= D_HD, bound, 0.0), axis=1, keepdims=True))

    fast = []
    for qc, shift in zip(qs, shifts):
        s = jnp.where(tail_ok, lax.dot_general(qc, kt, _NT, preferred_element_type=F32), NEG)
        e = jnp.exp2(s - shift)
        fast += [jnp.sum(e, axis=1, keepdims=True), jnp.dot(e.astype(BF16), vt, preferred_element_type=F32)]

    def fast_body(j, st):
        rows = pl.ds(pl.multiple_of(j * tk, tk), tk)
        kc = k_ref[0, rows, :]
        vc = v_ref[0, rows, :]
        new = []
        for c, (qc, shift) in enumerate(zip(qs, shifts)):
            l, acc = st[2 * c:2 * c + 2]
            e = jnp.exp2(lax.dot_general(qc, kc, _NT, preferred_element_type=F32) - shift)
            new += [l + jnp.sum(e, axis=1, keepdims=True),
                    acc + jnp.dot(e.astype(BF16), vc, preferred_element_type=F32)]
        return tuple(new)

    l1, acc1, l2, acc2 = lax.fori_loop(0, seq // tk, fast_body, tuple(fast))
    emit(acc1, l1, acc2, l2)

    @pl.when(jnp.logical_not(jnp.min(jnp.minimum(l1, l2)) >= 2.0 ** ATTN_MIN_LOG2_SUM))
    def _():
        state = []
        for qc in qs:
            s = jnp.where(tail_ok, lax.dot_general(qc, kt, _NT, preferred_element_type=F32), NEG)
            m = jnp.max(s, axis=1, keepdims=True)
            e = jnp.exp2(s - m)
            state += [m, jnp.sum(e, axis=1, keepdims=True),
                      jnp.dot(e.astype(BF16), vt, preferred_element_type=F32)]

        def body(j, st):
            rows = pl.ds(pl.multiple_of(j * tk, tk), tk)
            kc = k_ref[0, rows, :]
            vc = v_ref[0, rows, :]
            new = []
            for c, qc in enumerate(qs):
                m, l, acc = st[3 * c:3 * c + 3]
                s = lax.dot_general(qc, kc, _NT, preferred_element_type=F32)
                m_new = jnp.maximum(m, jnp.max(s, axis=1, keepdims=True))
                a = jnp.exp2(m - m_new)
                e = jnp.exp2(s - m_new)
                new += [m_new, a * l + jnp.sum(e, axis=1, keepdims=True),
                        a * acc + jnp.dot(e.astype(BF16), vc, preferred_element_type=F32)]
            return tuple(new)

        _, x1, a1, _, x2, a2 = lax.fori_loop(0, seq // tk, body, tuple(state))
        emit(a1, x1, a2, x2)


def _attn(proj, lam, gk, gs, seq, lambda_init, name):
    b, lp, _ = proj.shape
    tq = _row_tile(lp, ATTN_TQ)
    full = lambda i, h, j: (0, 0)
    return pl.pallas_call(
        functools.partial(_attn_kernel, seq=seq, tq=tq, lambda_init=lambda_init),
        grid=(b, D_HEADS, lp // tq),
        in_specs=[
            pl.BlockSpec((1, tq, LANES), lambda i, h, j: (i, j, h)),
            pl.BlockSpec((1, lp, LANES), lambda i, h, j: (i, 0, D_HEADS + h)),
            pl.BlockSpec((1, lp, LANES), lambda i, h, j: (i, 0, 2 * D_HEADS + h)),
            pl.BlockSpec((4, D_HD), full),
            pl.BlockSpec((1, D_HD), full),
            pl.BlockSpec((1, LANES), full),
        ],
        out_specs=pl.BlockSpec((1, tq, LANES), lambda i, h, j: (i, j, h)),
        out_shape=jax.ShapeDtypeStruct((b, lp, D_V_W), BF16),
        compiler_params=_params("arbitrary", "arbitrary", "arbitrary"),
        name=name,
    )(proj, proj, proj, lam, gk, gs)


def _rope_tables(seq, g_q, g_k):
    lp = seq + TAIL
    r = jnp.arange(lp)
    pos = jnp.where(r < seq, r + N_META, jnp.maximum(r - seq - M_PAD, 0)).astype(F32)
    half = ROPE_DIM // 2
    inv = ROPE_THETA ** (-jnp.arange(0, ROPE_DIM, 2, dtype=F32) / ROPE_DIM)
    ang = pos[:, None] * inv[None, :]
    cos, sin = jnp.cos(ang), jnp.sin(ang)
    rest = D_HD - ROPE_DIM

    def fold(g, scale):
        g = g.astype(F32) * scale
        cos_t = jnp.concatenate([cos * g[:half], cos * g[half:ROPE_DIM],
                                 jnp.broadcast_to(g[ROPE_DIM:], (lp, rest))], axis=1)
        sin_dn = jnp.concatenate([-sin * g[half:ROPE_DIM], jnp.zeros((lp, half + rest), F32)], axis=1)
        sin_up = jnp.concatenate([jnp.zeros((lp, half), F32), sin * g[:half],
                                  jnp.zeros((lp, rest), F32)], axis=1)
        return jnp.stack([jnp.tile(t, (1, 2)) for t in (cos_t, sin_dn, sin_up)])

    return jnp.stack([fold(g_q, D_HD ** -0.5 * math.log2(math.e)), fold(g_k, 1.0)])


def _trunk(x, meta_tokens, norm_mix, norm_ffn, m_w_main, m_w_gate, m_b_gate, m_g_head, m_w_out,
           d_w_in, d_g_q, d_g_k, d_lam, d_g_sub, d_w_out, ffn_w_in, ffn_w_out, tag):
    b, seq, d = x.shape
    lp = seq + TAIL
    rows = b * lp
    meta = jnp.broadcast_to(meta_tokens[None].astype(x.dtype), (b, N_META, d))
    h = jnp.concatenate([x, jnp.zeros((b, M_PAD, d), x.dtype), meta], axis=1)
    for i in range(DEPTH):
        j = i // 2
        gmix = norm_mix[i][None]
        hrows = h.reshape(rows, d)
        if i % 2 == 0:
            *qkvo, gates = _m_in(hrows, gmix, m_w_main[j], m_w_gate[j], m_b_gate[j], f"m_in_{tag}{i}")
            grow, gcol = _gate_prep(gates.reshape(b, lp, LANES), seq, f"m_prep_{tag}{i}")
            qkvo = [t.reshape(M_HEADS, b, lp, t.shape[-1]) for t in qkvo]
            mixed = _mlstm(*qkvo, grow, gcol, m_g_head[j][None], seq, f"m_cell_{tag}{i}")
            w_mix_out = m_w_out[j]
        else:
            lambda_init = 0.8 - 0.6 * math.exp(-0.3 * i)
            tables = _rope_tables(seq, d_g_q[j], d_g_k[j])
            proj = _qkv_rope(hrows, gmix, d_w_in[j], tables, lp, f"d_in_{tag}{i}")
            mixed = _attn(proj.reshape(b, lp, 3 * D_QK_W), d_lam[j], d_g_k[j].astype(F32)[None],
                          d_g_sub[j], seq, lambda_init, f"d_attn_{tag}{i}")
            w_mix_out = d_w_out[j]
        h = _ffn(mixed, w_mix_out, h, norm_ffn[i][None], ffn_w_in[i], ffn_w_out[i], f"ffn_{tag}{i}",
                 out_seq=seq if i == DEPTH - 1 else None)
    return h


def kernel(x_prompt, x_sample, meta_tokens, norm_mix, norm_ffn, m_w_in, m_b_gate, m_g_head, m_w_out,
           d_w_in, d_g_q, d_g_k, d_lam_q1, d_lam_k1, d_lam_q2, d_lam_k2, d_g_sub, d_w_out,
           ffn_w_in, ffn_w_out):
    n_gate = 4 * M_HEADS
    m_w_main = m_w_in[:, :, :M_MAIN_W].astype(BF16)
    m_w_gate = jnp.pad(m_w_in[:, :, M_MAIN_W:], ((0, 0), (0, 0), (0, LANES - n_gate))).astype(BF16)
    b_gate = jnp.pad(m_b_gate.astype(F32), ((0, 0), (0, LANES - n_gate)))[:, None, :]
    d_lam = jnp.stack([d_lam_q1, d_lam_k1, d_lam_q2, d_lam_k2], axis=1).astype(F32)
    args = (meta_tokens, norm_mix.astype(F32), norm_ffn.astype(F32), m_w_main, m_w_gate, b_gate,
            m_g_head.astype(F32), m_w_out.astype(BF16), d_w_in.astype(BF16), d_g_q, d_g_k,
            d_lam, d_g_sub.astype(F32)[:, None, :], d_w_out.astype(BF16), ffn_w_in.astype(BF16),
            ffn_w_out.astype(BF16))
    return (_trunk(x_prompt, *args, "p"), _trunk(x_sample, *args, "s"))
```

```python
import functools
import math

import jax
import jax.numpy as jnp
from jax import lax
from jax.experimental import pallas as pl
from jax.experimental.pallas import tpu as pltpu

F32 = jnp.float32
BF16 = jnp.bfloat16

D_MODEL = 1024
DEPTH = 4
N_META = 16
RMS_EPS = 1e-6
NEG = -1e30
M_HEADS = 4
M_DK = 128
M_DV = 256
M_CHUNK = 64
M_XCHUNK = 256
M_PAD = M_CHUNK - N_META
M_QK_W = M_HEADS * M_DK
M_V_W = M_HEADS * M_DV
M_MAIN_W = 2 * M_QK_W + 2 * M_V_W
D_HEADS = 8
D_HD = 64
D_QK_W = D_HEADS * 2 * D_HD
D_V_W = D_HEADS * 2 * D_HD
ROPE_THETA = 500000.0
ROPE_DIM = D_HD // 4
FF = 2816
TAIL = M_CHUNK
LANES = 128
VMEM_LIMIT = 56 * 1024 * 1024


def _params(*sem):
    return pltpu.CompilerParams(dimension_semantics=sem, vmem_limit_bytes=VMEM_LIMIT)


def _row_tile(rows, target):
    best = 16
    for t in range(16, target + 1, 16):
        if rows % t == 0:
            best = t
    return best


def _rms_rows(x, g):
    ms = jnp.mean(x * x, axis=-1, keepdims=True)
    return x * lax.rsqrt(ms + RMS_EPS) * g


MM_COLS = 1024


def _m_in_kernel(x_ref, g_ref, w_ref, wg_ref, bg_ref, q_ref, k_ref, v_ref, o_ref, og_ref):
    xn = _rms_rows(x_ref[...], g_ref[...]).astype(BF16)
    outs = ((q_ref, M_DK), (k_ref, M_DK), (v_ref, M_DV), (o_ref, M_DV))
    col = 0
    for ref, width in outs:
        total = M_HEADS * width
        cols = min(MM_COLS, total)
        for c0 in range(0, total, cols):
            acc = jnp.dot(xn, w_ref[:, col + c0:col + c0 + cols],
                          preferred_element_type=F32).astype(BF16)
            for t in range(cols // width):
                ref[c0 // width + t] = acc[:, t * width:(t + 1) * width]
        col += total
    og_ref[...] = jnp.dot(xn, wg_ref[...], preferred_element_type=F32) + bg_ref[...]


def _m_in(x, g, w, wg, bg, name):
    rows, d = x.shape
    tm = _row_tile(rows, 1056)
    const = lambda i: (0, 0)
    head_major = lambda width: pl.BlockSpec((M_HEADS, tm, width), lambda i: (0, i, 0))
    widths = (M_DK, M_DK, M_DV, M_DV)
    return pl.pallas_call(
        _m_in_kernel,
        grid=(rows // tm,),
        in_specs=[
            pl.BlockSpec((tm, d), lambda i: (i, 0)),
            pl.BlockSpec((1, d), const),
            pl.BlockSpec((d, M_MAIN_W), const, pipeline_mode=pl.Buffered(1)),
            pl.BlockSpec((d, LANES), const, pipeline_mode=pl.Buffered(1)),
            pl.BlockSpec((1, LANES), const),
        ],
        out_specs=[head_major(wd) for wd in widths] + [pl.BlockSpec((tm, LANES), lambda i: (i, 0))],
        out_shape=[jax.ShapeDtypeStruct((M_HEADS, rows, wd), BF16) for wd in widths]
        + [jax.ShapeDtypeStruct((rows, LANES), F32)],
        compiler_params=_params("arbitrary"),
        name=name,
    )(x, g, w, wg, bg)


def _group_ones(width, group):
    gi = lax.broadcasted_iota(jnp.int32, (width, width), 0) // group
    gj = lax.broadcasted_iota(jnp.int32, (width, width), 1) // group
    return jnp.where(gi == gj, 1.0, 0.0).astype(BF16)


def _qkv_rope_kernel(x_ref, g_ref, w_ref, tab_ref, o_ref):
    xn = _rms_rows(x_ref[...], g_ref[...]).astype(BF16)
    wide = 2 * LANES
    group = _group_ones(wide, D_HD)
    half = ROPE_DIM // 2
    for j in range(3):
        acc = jnp.dot(xn, w_ref[:, j * D_QK_W:(j + 1) * D_QK_W], preferred_element_type=F32)
        if j == 2:
            o_ref[:, j * D_QK_W:(j + 1) * D_QK_W] = acc.astype(BF16)
            continue
        for p in range(D_QK_W // wide):
            x2 = acc[:, p * wide:(p + 1) * wide]
            ssq = jnp.dot((x2 * x2).astype(BF16), group, preferred_element_type=F32)
            xs = x2 * lax.rsqrt(ssq * (1.0 / D_HD) + RMS_EPS)
            for hf in range(2):
                xh = xs[:, hf * LANES:(hf + 1) * LANES]
                out = (xh * tab_ref[j, 0] + pltpu.roll(xh, LANES - half, 1) * tab_ref[j, 1]
                       + pltpu.roll(xh, half, 1) * tab_ref[j, 2])
                c0 = j * D_QK_W + p * wide + hf * LANES
                o_ref[:, c0:c0 + LANES] = out.astype(BF16)


def _qkv_rope(x, g, w, tables, lp, name):
    rows, d = x.shape
    n = w.shape[1]
    tm = _row_tile(lp, 1056)
    nb = lp // tm
    return pl.pallas_call(
        _qkv_rope_kernel,
        grid=(rows // tm,),
        in_specs=[
            pl.BlockSpec((tm, d), lambda i: (i, 0)),
            pl.BlockSpec((1, d), lambda i: (0, 0)),
            pl.BlockSpec((d, n), lambda i: (0, 0), pipeline_mode=pl.Buffered(1)),
            pl.BlockSpec((2, 3, tm, LANES), lambda i: (0, 0, i % nb, 0)),
        ],
        out_specs=pl.BlockSpec((tm, n), lambda i: (i, 0)),
        out_shape=jax.ShapeDtypeStruct((rows, n), BF16),
        compiler_params=_params("arbitrary"),
        name=name,
    )(x, g, w, tables)


FF_CHUNK = 512
FFN_ROWS = 512


def _ffn_kernel(a_ref, wm_ref, h_ref, g_ref, wi_ref, wo_ref, o_ref, xn_ref, acc_ref):
    if len(a_ref.shape) == 3:
        a = jnp.concatenate([a_ref[hd] for hd in range(a_ref.shape[0])], axis=1)
    else:
        a = a_ref[...]
    x = h_ref[...] + jnp.dot(a, wm_ref[...], preferred_element_type=F32)
    xn_ref[...] = _rms_rows(x, g_ref[...]).astype(BF16)
    acc_ref[...] = x
    for c0 in range(0, FF, FF_CHUNK):
        c1 = min(c0 + FF_CHUNK, FF)
        xn = xn_ref[...]
        gate = jnp.dot(xn, wi_ref[:, c0:c1], preferred_element_type=F32)
        up = jnp.dot(xn, wi_ref[:, FF + c0:FF + c1], preferred_element_type=F32)
        act = (gate * (1.0 / (1.0 + jnp.exp(-gate))) * up).astype(BF16)
        acc_ref[...] += jnp.dot(act, wo_ref[c0:c1, :], preferred_element_type=F32)
    o_ref[...] = acc_ref[...]


def _ffn(a, wm, h, g, wi, wo, name, out_seq=None):
    b, lp, d = h.shape
    const = lambda *_: (0, 0)
    weights = [
        pl.BlockSpec((d, d), const, pipeline_mode=pl.Buffered(1)),
        pl.BlockSpec((1, d), const),
        pl.BlockSpec((d, 2 * FF), const, pipeline_mode=pl.Buffered(1)),
        pl.BlockSpec((FF, d), const, pipeline_mode=pl.Buffered(1)),
    ]
    if out_seq is None:
        rows = b * lp
        tm = _row_tile(rows, FFN_ROWS)
        grid = (rows // tm,)
        tile = pl.BlockSpec((tm, d), lambda i: (i, 0))
        h = h.reshape(rows, d)
        if a.ndim == 4:
            a = a.reshape(a.shape[0], rows, a.shape[3])
            a_tile = pl.BlockSpec((a.shape[0], tm, a.shape[2]), lambda i: (0, i, 0))
        else:
            a, a_tile = a.reshape(rows, d), tile
        out_shape = jax.ShapeDtypeStruct((rows, d), F32)
        sem = ("arbitrary",)
    else:
        tm = _row_tile(out_seq, FFN_ROWS)
        grid = (b, out_seq // tm)
        tile = a_tile = pl.BlockSpec((None, tm, d), lambda i, j: (i, j, 0))
        if a.ndim == 4:
            a_tile = pl.BlockSpec((a.shape[0], None, tm, a.shape[3]), lambda i, j: (0, i, j, 0))
        out_shape = jax.ShapeDtypeStruct((b, out_seq, d), F32)
        sem = ("arbitrary", "arbitrary")
    out = pl.pallas_call(
        _ffn_kernel,
        grid=grid,
        in_specs=[a_tile, weights[0], tile, weights[1], weights[2], weights[3]],
        out_specs=tile,
        out_shape=out_shape,
        scratch_shapes=[pltpu.VMEM((tm, d), BF16), pltpu.VMEM((tm, d), F32)],
        compiler_params=_params(*sem),
        name=name,
    )(a, wm, h, g, wi, wo)
    return out if out_seq is not None else out.reshape(b, lp, d)


def _split3(x):
    hi = x.astype(BF16)
    r1 = x - hi.astype(F32)
    mid = r1.astype(BF16)
    lo = (r1 - mid.astype(F32)).astype(BF16)
    return hi, mid, lo


def _dot_exact01(m01, x, dims):
    out = None
    for part in _split3(x):
        t = lax.dot_general(m01, part, dims, preferred_element_type=F32)
        out = t if out is None else out + t
    return out


_NN = (((1,), (0,)), ((), ()))
_NT = (((1,), (1,)), ((), ()))
_TN = (((0,), (0,)), ((), ()))


def _gate_prep_kernel(g_ref, row_ref, col_ref, *, seq):
    rb = M_XCHUNK
    r = pl.program_id(1) * rb + lax.broadcasted_iota(jnp.int32, (rb, LANES), 0)
    valid = jnp.logical_or(r < seq, jnp.logical_and(r >= seq + M_PAD, r < seq + TAIL))
    g = jnp.where(valid, g_ref[0], 0.0)
    logf = jnp.where(valid, jnp.minimum(g, 0.0) - jnp.log(1.0 + jnp.exp(-jnp.abs(g))), 0.0)
    ig = jnp.where(valid, g, NEG)
    ti = lax.broadcasted_iota(jnp.int32, (rb, rb), 0)
    si = lax.broadcasted_iota(jnp.int32, (rb, rb), 1)
    pre_m = jnp.where(si <= ti, 1.0, 0.0).astype(BF16)
    pre = _dot_exact01(pre_m, logf, _NN)
    suf = pre[rb - 1:rb, :] - pre + logf
    lane = lax.broadcasted_iota(jnp.int32, (rb, LANES), 1)
    pre_dn = pltpu.roll(pre, LANES - M_HEADS, 1)
    suf_dn = pltpu.roll(suf, LANES - M_HEADS, 1)
    row = jnp.where(lane < 4, ig - pre_dn,
                    jnp.where(lane < 8, pre,
                              jnp.where(lane < 12, ig - suf_dn, suf)))
    row = jnp.where(lane < 16, row, 0.0)
    row_ref[0] = row
    er = lax.broadcasted_iota(jnp.int32, (8, LANES), 0)
    el = lax.broadcasted_iota(jnp.int32, (8, LANES), 1)
    pick = jnp.where(el == jnp.where(er < 4, er, er + 4), 1.0, 0.0).astype(BF16)
    col_ref[0, 0] = _dot_exact01(pick, row, _NT)


def _gate_prep(g, seq, name):
    b, lp, _ = g.shape
    nblk = seq // M_XCHUNK + 1
    return pl.pallas_call(
        functools.partial(_gate_prep_kernel, seq=seq),
        grid=(b, nblk),
        in_specs=[pl.BlockSpec((1, M_XCHUNK, LANES), lambda i, j: (i, j, 0))],
        out_specs=[
            pl.BlockSpec((1, M_XCHUNK, LANES), lambda i, j: (i, j, 0)),
            pl.BlockSpec((1, 1, 8, M_XCHUNK), lambda i, j: (i, j, 0, 0)),
        ],
        out_shape=[
            jax.ShapeDtypeStruct((b, lp, LANES), F32),
            jax.ShapeDtypeStruct((b, nblk, 8, M_XCHUNK), F32),
        ],
        compiler_params=_params("arbitrary", "arbitrary"),
        name=name,
    )(g)


def _lane_pick(x, idx):
    lane = lax.broadcasted_iota(jnp.int32, x.shape, 1)
    return jnp.sum(jnp.where(lane == idx, x, 0.0), axis=1, keepdims=True)


def _chunk_start(c):
    return c * M_XCHUNK if isinstance(c, int) else pl.multiple_of(c * M_XCHUNK, M_XCHUNK)


def _mlstm_kernel(q_ref, k_ref, v_ref, o_ref, gr_ref, gc_ref, gh_ref, out_ref,
                  hf_ref, hb_ref, cn_ref, m_ref, *, seq):
    head = pl.program_id(1)
    nx = seq // M_XCHUNK
    qk_scale = M_DK ** -0.5
    cn_ref[...] = jnp.zeros_like(cn_ref)
    m_ref[...] = jnp.zeros_like(m_ref)

    def step(d, c, size):
        def wide(x, n):
            return x[:, :n] if n < LANES else jnp.concatenate([x] * (n // LANES), axis=1)

        rows = pl.ds(_chunk_start(c), size)
        ti = lax.broadcasted_iota(jnp.int32, (size, size), 0)
        si = lax.broadcasted_iota(jnp.int32, (size, size), 1)
        q = q_ref[rows, :]
        k = k_ref[rows, :]
        v1 = jnp.concatenate([v_ref[rows, :], jnp.ones((size, LANES), BF16)], axis=1)
        beta = jnp.broadcast_to(_lane_pick(gr_ref[0, rows, :], 8 * d + 4 + head), (size, LANES))
        alpha_c = gc_ref[0, c, pl.ds(4 * d + head, 1), :][:, :size]
        m_prev = m_ref[d]
        cn = cn_ref[d]
        keep = (si <= ti) if d == 0 else (si >= ti)
        dmat = jnp.where(keep, wide(beta, size) + alpha_c, -jnp.inf)
        m_inter = beta + m_prev
        m_comb = jnp.maximum(jnp.broadcast_to(jnp.max(dmat, axis=1, keepdims=True), (size, LANES)),
                             m_inter)
        s = lax.dot_general(q, k, _NT, preferred_element_type=F32)
        w = jnp.exp(dmat - wide(m_comb, size)) * s
        scale = jnp.exp(m_inter - m_comb)
        both = (jnp.dot(w.astype(BF16), v1, preferred_element_type=F32)
                + wide(scale, M_DV + LANES)
                * jnp.dot(q, cn.astype(BF16), preferred_element_type=F32)) * qk_scale
        inv = 1.0 / jnp.maximum(jnp.abs(both[:, M_DV:]), jnp.exp(-m_comb))
        hval = both[:, :M_DV] * wide(inv, M_DV)
        if d == 0:
            hf_ref[rows, :] = hval
            tot = beta[size - 1:size]
        else:
            hb_ref[rows, :] = hval
            tot = beta[0:1]
        a_max = jnp.max(alpha_c, axis=1, keepdims=True)
        m_new = jnp.maximum(tot + m_prev, tot + a_max)
        decay = jnp.exp(tot + m_prev - m_new)
        wk = jnp.exp(alpha_c + wide(tot - m_new, size))
        kw = (k.T.astype(F32) * wk).astype(BF16)
        cn_ref[d] = wide(decay, M_DV + LANES) * cn + jnp.dot(kw, v1, preferred_element_type=F32)
        m_ref[d] = m_new

    step(0, nx, TAIL)
    step(1, nx - 1, M_XCHUNK)

    def body(i, carry):
        step(0, i - 1, M_XCHUNK)
        step(1, nx - 1 - i, M_XCHUNK)
        return carry

    lax.fori_loop(1, nx, body, 0, unroll=True)
    step(0, nx - 1, M_XCHUNK)
    step(1, nx, TAIL)

    def finish(c, size):
        rows = pl.ds(_chunk_start(c), size)
        hsum = hf_ref[rows, :] + hb_ref[rows, :]
        og = o_ref[rows, :].astype(F32)
        out_ref[rows, :] = (_rms_rows(hsum, gh_ref[...]) * (1.0 / (1.0 + jnp.exp(-og)))).astype(BF16)

    def finish_body(c, carry):
        finish(c, M_XCHUNK)
        return carry

    lax.fori_loop(0, nx, finish_body, 0)
    finish(nx, TAIL)


def _mlstm(q, k, v, o, grow, gcol, g_head, seq, name):
    _, b, lp, _ = q.shape
    nblk = seq // M_XCHUNK + 1
    per_head = lambda width: pl.BlockSpec((None, None, lp, width), lambda i, h: (h, i, 0, 0))
    return pl.pallas_call(
        functools.partial(_mlstm_kernel, seq=seq),
        grid=(b, M_HEADS),
        in_specs=[
            per_head(M_DK),
            per_head(M_DK),
            per_head(M_DV),
            per_head(M_DV),
            pl.BlockSpec((1, lp, LANES), lambda i, h: (i, 0, 0)),
            pl.BlockSpec((1, nblk, 8, M_XCHUNK), lambda i, h: (i, 0, 0, 0)),
            pl.BlockSpec((1, M_DV), lambda i, h: (0, 0)),
        ],
        out_specs=per_head(M_DV),
        out_shape=jax.ShapeDtypeStruct((M_HEADS, b, lp, M_DV), BF16),
        scratch_shapes=[
            pltpu.VMEM((lp, M_DV), F32),
            pltpu.VMEM((lp, M_DV), F32),
            pltpu.VMEM((2, M_DK, M_DV + LANES), F32),
            pltpu.VMEM((2, 1, LANES), F32),
        ],
        compiler_params=_params("arbitrary", "arbitrary"),
        name=name,
    )(q, k, v, o, grow, gcol, g_head)


ATTN_TK = 2048
ATTN_TQ = 1056


ATTN_MIN_LOG2_SUM = -60.0
ATTN_BOUND_SLACK = 1.02


def _attn_kernel(q_ref, k_ref, v_ref, lam_ref, gk_ref, gs_ref, out_ref, *, seq, tq, lambda_init):
    lp = seq + TAIL
    tk = min(ATTN_TK, seq)
    q = q_ref[0]
    lane = lax.broadcasted_iota(jnp.int32, (tq, LANES), 1)
    zero = jnp.zeros_like(q)
    qs = (jnp.where(lane < D_HD, q, zero), jnp.where(lane >= D_HD, q, zero))
    kt = k_ref[0, seq:lp, :]
    vt = v_ref[0, seq:lp, :]
    tail_ok = lax.broadcasted_iota(jnp.int32, (tq, TAIL), 1) >= M_PAD
    lam_p = lam_ref[...]
    lam = (jnp.exp(jnp.sum(lam_p[0:1] * lam_p[1:2], axis=1, keepdims=True))
           - jnp.exp(jnp.sum(lam_p[2:3] * lam_p[3:4], axis=1, keepdims=True)) + lambda_init)

    def emit(acc1, l1, acc2, l2):
        o = acc1 / l1 - lam * (acc2 / l2)
        out_ref[0] = (_rms_rows(o, gs_ref[...]) * (1.0 - lambda_init)).astype(BF16)

    group = _group_ones(LANES, D_HD)
    ksq = D_HD * jnp.max(gk_ref[...] * gk_ref[...], axis=1, keepdims=True)
    qf = q.astype(F32)
    qsq = jnp.dot((qf * qf).astype(BF16), group, preferred_element_type=F32)
    bound = jnp.sqrt(qsq * ksq) * ATTN_BOUND_SLACK
    shifts = (jnp.max(jnp.where(lane < D_HD, bound, 0.0), axis=1, keepdims=True),
              jnp.max(jnp.where(lane >= D_HD, bound, 0.0), axis=1, keepdims=True))

    fast = []
    for qc, shift in zip(qs, shifts):
        s = jnp.where(tail_ok, lax.dot_general(qc, kt, _NT, preferred_element_type=F32), NEG)
        e = jnp.exp2(s - shift)
        fast += [jnp.sum(e, axis=1, keepdims=True), jnp.dot(e.astype(BF16), vt, preferred_element_type=F32)]

    def fast_body(j, st):
        rows = pl.ds(pl.multiple_of(j * tk, tk), tk)
        kc = k_ref[0, rows, :]
        vc = v_ref[0, rows, :]
        new = []
        for c, (qc, shift) in enumerate(zip(qs, shifts)):
            l, acc = st[2 * c:2 * c + 2]
            e = jnp.exp2(lax.dot_general(qc, kc, _NT, preferred_element_type=F32) - shift)
            new += [l + jnp.sum(e, axis=1, keepdims=True),
                    acc + jnp.dot(e.astype(BF16), vc, preferred_element_type=F32)]
        return tuple(new)

    l1, acc1, l2, acc2 = lax.fori_loop(0, seq // tk, fast_body, tuple(fast), unroll=True)
    emit(acc1, l1, acc2, l2)

    @pl.when(jnp.logical_not(jnp.min(jnp.minimum(l1, l2)) >= 2.0 ** ATTN_MIN_LOG2_SUM))
    def _():
        state = []
        for qc in qs:
            s = jnp.where(tail_ok, lax.dot_general(qc, kt, _NT, preferred_element_type=F32), NEG)
            m = jnp.max(s, axis=1, keepdims=True)
            e = jnp.exp2(s - m)
            state += [m, jnp.sum(e, axis=1, keepdims=True),
                      jnp.dot(e.astype(BF16), vt, preferred_element_type=F32)]

        def body(j, st):
            rows = pl.ds(pl.multiple_of(j * tk, tk), tk)
            kc = k_ref[0, rows, :]
            vc = v_ref[0, rows, :]
            new = []
            for c, qc in enumerate(qs):
                m, l, acc = st[3 * c:3 * c + 3]
                s = lax.dot_general(qc, kc, _NT, preferred_element_type=F32)
                m_new = jnp.maximum(m, jnp.max(s, axis=1, keepdims=True))
                a = jnp.exp2(m - m_new)
                e = jnp.exp2(s - m_new)
                new += [m_new, a * l + jnp.sum(e, axis=1, keepdims=True),
                        a * acc + jnp.dot(e.astype(BF16), vc, preferred_element_type=F32)]
            return tuple(new)

        _, x1, a1, _, x2, a2 = lax.fori_loop(0, seq // tk, body, tuple(state))
        emit(a1, x1, a2, x2)


def _attn(proj, lam, gk, gs, seq, lambda_init, name):
    b, lp, _ = proj.shape
    tq = _row_tile(lp, ATTN_TQ)
    full = lambda i, h, j: (0, 0)
    return pl.pallas_call(
        functools.partial(_attn_kernel, seq=seq, tq=tq, lambda_init=lambda_init),
        grid=(b, D_HEADS, lp // tq),
        in_specs=[
            pl.BlockSpec((1, tq, LANES), lambda i, h, j: (i, j, h)),
            pl.BlockSpec((1, lp, LANES), lambda i, h, j: (i, 0, D_HEADS + h)),
            pl.BlockSpec((1, lp, LANES), lambda i, h, j: (i, 0, 2 * D_HEADS + h)),
            pl.BlockSpec((4, D_HD), full),
            pl.BlockSpec((1, D_HD), full),
            pl.BlockSpec((1, LANES), full),
        ],
        out_specs=pl.BlockSpec((1, tq, LANES), lambda i, h, j: (i, j, h)),
        out_shape=jax.ShapeDtypeStruct((b, lp, D_V_W), BF16),
        compiler_params=_params("arbitrary", "arbitrary", "arbitrary"),
        name=name,
    )(proj, proj, proj, lam, gk, gs)


def _rope_tables(seq, g_q, g_k):
    lp = seq + TAIL
    r = jnp.arange(lp)
    pos = jnp.where(r < seq, r + N_META, jnp.maximum(r - seq - M_PAD, 0)).astype(F32)
    half = ROPE_DIM // 2
    inv = ROPE_THETA ** (-jnp.arange(0, ROPE_DIM, 2, dtype=F32) / ROPE_DIM)
    ang = pos[:, None] * inv[None, :]
    cos, sin = jnp.cos(ang), jnp.sin(ang)
    rest = D_HD - ROPE_DIM

    def fold(g, scale):
        g = g.astype(F32) * scale
        cos_t = jnp.concatenate([cos * g[:half], cos * g[half:ROPE_DIM],
                                 jnp.broadcast_to(g[ROPE_DIM:], (lp, rest))], axis=1)
        sin_dn = jnp.concatenate([-sin * g[half:ROPE_DIM], jnp.zeros((lp, half + rest), F32)], axis=1)
        sin_up = jnp.concatenate([jnp.zeros((lp, half), F32), sin * g[:half],
                                  jnp.zeros((lp, rest), F32)], axis=1)
        return jnp.stack([jnp.tile(t, (1, 2)) for t in (cos_t, sin_dn, sin_up)])

    return jnp.stack([fold(g_q, D_HD ** -0.5 * math.log2(math.e)), fold(g_k, 1.0)])


def _trunk(x, meta_tokens, norm_mix, norm_ffn, m_w_main, m_w_gate, m_b_gate, m_g_head, m_w_out,
           d_w_in, d_g_q, d_g_k, d_lam, d_g_sub, d_w_out, ffn_w_in, ffn_w_out, tag):
    b, seq, d = x.shape
    lp = seq + TAIL
    rows = b * lp
    meta = jnp.broadcast_to(meta_tokens[None].astype(x.dtype), (b, N_META, d))
    h = jnp.concatenate([x, jnp.zeros((b, M_PAD, d), x.dtype), meta], axis=1)
    for i in range(DEPTH):
        j = i // 2
        gmix = norm_mix[i][None]
        hrows = h.reshape(rows, d)
        if i % 2 == 0:
            *qkvo, gates = _m_in(hrows, gmix, m_w_main[j], m_w_gate[j], m_b_gate[j], f"m_in_{tag}{i}")
            grow, gcol = _gate_prep(gates.reshape(b, lp, LANES), seq, f"m_prep_{tag}{i}")
            qkvo = [t.reshape(M_HEADS, b, lp, t.shape[-1]) for t in qkvo]
            mixed = _mlstm(*qkvo, grow, gcol, m_g_head[j][None], seq, f"m_cell_{tag}{i}")
            w_mix_out = m_w_out[j]
        else:
            lambda_init = 0.8 - 0.6 * math.exp(-0.3 * i)
            tables = _rope_tables(seq, d_g_q[j], d_g_k[j])
            proj = _qkv_rope(hrows, gmix, d_w_in[j], tables, lp, f"d_in_{tag}{i}")
            mixed = _attn(proj.reshape(b, lp, 3 * D_QK_W), d_lam[j], d_g_k[j].astype(F32)[None],
                          d_g_sub[j], seq, lambda_init, f"d_attn_{tag}{i}")
            w_mix_out = d_w_out[j]
        h = _ffn(mixed, w_mix_out, h, norm_ffn[i][None], ffn_w_in[i], ffn_w_out[i], f"ffn_{tag}{i}",
                 out_seq=seq if i == DEPTH - 1 else None)
    return h


def kernel(x_prompt, x_sample, meta_tokens, norm_mix, norm_ffn, m_w_in, m_b_gate, m_g_head, m_w_out,
           d_w_in, d_g_q, d_g_k, d_lam_q1, d_lam_k1, d_lam_q2, d_lam_k2, d_g_sub, d_w_out,
           ffn_w_in, ffn_w_out):
    n_gate = 4 * M_HEADS
    m_w_main = m_w_in[:, :, :M_MAIN_W].astype(BF16)
    m_w_gate = jnp.pad(m_w_in[:, :, M_MAIN_W:], ((0, 0), (0, 0), (0, LANES - n_gate))).astype(BF16)
    b_gate = jnp.pad(m_b_gate.astype(F32), ((0, 0), (0, LANES - n_gate)))[:, None, :]
    d_lam = jnp.stack([d_lam_q1, d_lam_k1, d_lam_q2, d_lam_k2], axis=1).astype(F32)
    args = (meta_tokens, norm_mix.astype(F32), norm_ffn.astype(F32), m_w_main, m_w_gate, b_gate,
            m_g_head.astype(F32), m_w_out.astype(BF16), d_w_in.astype(BF16), d_g_q, d_g_k,
            d_lam, d_g_sub.astype(F32)[:, None, :], d_w_out.astype(BF16), ffn_w_in.astype(BF16),
            ffn_w_out.astype(BF16))
    return (_trunk(x_prompt, *args, "p"), _trunk(x_sample, *args, "s"))
```

```python
import functools
import math

import jax
import jax.numpy as jnp
from jax import lax
from jax.experimental import pallas as pl
from jax.experimental.pallas import tpu as pltpu

F32 = jnp.float32
BF16 = jnp.bfloat16

D_MODEL = 1024
DEPTH = 4
N_META = 16
RMS_EPS = 1e-6
NEG = -1e30
M_HEADS = 4
M_DK = 128
M_DV = 256
M_CHUNK = 64
M_XCHUNK = 256
M_PAD = M_CHUNK - N_META
M_QK_W = M_HEADS * M_DK
M_V_W = M_HEADS * M_DV
M_MAIN_W = 2 * M_QK_W + 2 * M_V_W
D_HEADS = 8
D_HD = 64
D_QK_W = D_HEADS * 2 * D_HD
D_V_W = D_HEADS * 2 * D_HD
ROPE_THETA = 500000.0
ROPE_DIM = D_HD // 4
FF = 2816
TAIL = M_CHUNK
LANES = 128
VMEM_LIMIT = 56 * 1024 * 1024


def _params(*sem):
    return pltpu.CompilerParams(dimension_semantics=sem, vmem_limit_bytes=VMEM_LIMIT)


def _row_tile(rows, target):
    best = 16
    for t in range(16, target + 1, 16):
        if rows % t == 0:
            best = t
    return best


def _rms_rows(x, g):
    ms = jnp.mean(x * x, axis=-1, keepdims=True)
    return x * lax.rsqrt(ms + RMS_EPS) * g


MM_COLS = 1024


def _m_in_kernel(x_ref, g_ref, w_ref, wg_ref, bg_ref, q_ref, k_ref, v_ref, o_ref, og_ref):
    xn = _rms_rows(x_ref[...], g_ref[...]).astype(BF16)
    outs = ((q_ref, M_DK), (k_ref, M_DK), (v_ref, M_DV), (o_ref, M_DV))
    col = 0
    for ref, width in outs:
        total = M_HEADS * width
        cols = min(MM_COLS, total)
        for c0 in range(0, total, cols):
            acc = jnp.dot(xn, w_ref[:, col + c0:col + c0 + cols],
                          preferred_element_type=F32).astype(BF16)
            for t in range(cols // width):
                ref[c0 // width + t] = acc[:, t * width:(t + 1) * width]
        col += total
    og_ref[...] = jnp.dot(xn, wg_ref[...], preferred_element_type=F32) + bg_ref[...]


def _m_in(x, g, w, wg, bg, name):
    rows, d = x.shape
    tm = _row_tile(rows, 1056)
    const = lambda i: (0, 0)
    head_major = lambda width: pl.BlockSpec((M_HEADS, tm, width), lambda i: (0, i, 0))
    widths = (M_DK, M_DK, M_DV, M_DV)
    return pl.pallas_call(
        _m_in_kernel,
        grid=(rows // tm,),
        in_specs=[
            pl.BlockSpec((tm, d), lambda i: (i, 0)),
            pl.BlockSpec((1, d), const),
            pl.BlockSpec((d, M_MAIN_W), const, pipeline_mode=pl.Buffered(1)),
            pl.BlockSpec((d, LANES), const, pipeline_mode=pl.Buffered(1)),
            pl.BlockSpec((1, LANES), const),
        ],
        out_specs=[head_major(wd) for wd in widths] + [pl.BlockSpec((tm, LANES), lambda i: (i, 0))],
        out_shape=[jax.ShapeDtypeStruct((M_HEADS, rows, wd), BF16) for wd in widths]
        + [jax.ShapeDtypeStruct((rows, LANES), F32)],
        compiler_params=_params("arbitrary"),
        name=name,
    )(x, g, w, wg, bg)


def _group_ones(width, group):
    gi = lax.broadcasted_iota(jnp.int32, (width, width), 0) // group
    gj = lax.broadcasted_iota(jnp.int32, (width, width), 1) // group
    return jnp.where(gi == gj, 1.0, 0.0).astype(BF16)


def _qkv_rope_kernel(x_ref, g_ref, w_ref, tab_ref, o_ref):
    xn = _rms_rows(x_ref[...], g_ref[...]).astype(BF16)
    wide = 2 * LANES
    group = _group_ones(wide, D_HD)
    half = ROPE_DIM // 2
    for j in range(3):
        acc = jnp.dot(xn, w_ref[:, j * D_QK_W:(j + 1) * D_QK_W], preferred_element_type=F32)
        if j == 2:
            o_ref[:, j * D_QK_W:(j + 1) * D_QK_W] = acc.astype(BF16)
            continue
        for p in range(D_QK_W // wide):
            x2 = acc[:, p * wide:(p + 1) * wide]
            ssq = jnp.dot((x2 * x2).astype(BF16), group, preferred_element_type=F32)
            xs = x2 * lax.rsqrt(ssq * (1.0 / D_HD) + RMS_EPS)
            for hf in range(2):
                xh = xs[:, hf * LANES:(hf + 1) * LANES]
                out = (xh * tab_ref[j, 0] + pltpu.roll(xh, LANES - half, 1) * tab_ref[j, 1]
                       + pltpu.roll(xh, half, 1) * tab_ref[j, 2])
                c0 = j * D_QK_W + p * wide + hf * LANES
                o_ref[:, c0:c0 + LANES] = out.astype(BF16)


def _qkv_rope(x, g, w, tables, lp, name):
    rows, d = x.shape
    n = w.shape[1]
    tm = _row_tile(lp, 1056)
    nb = lp // tm
    return pl.pallas_call(
        _qkv_rope_kernel,
        grid=(rows // tm,),
        in_specs=[
            pl.BlockSpec((tm, d), lambda i: (i, 0)),
            pl.BlockSpec((1, d), lambda i: (0, 0)),
            pl.BlockSpec((d, n), lambda i: (0, 0), pipeline_mode=pl.Buffered(1)),
            pl.BlockSpec((2, 3, tm, LANES), lambda i: (0, 0, i % nb, 0)),
        ],
        out_specs=pl.BlockSpec((tm, n), lambda i: (i, 0)),
        out_shape=jax.ShapeDtypeStruct((rows, n), BF16),
        compiler_params=_params("arbitrary"),
        name=name,
    )(x, g, w, tables)


FF_CHUNK = 512
FFN_ROWS = 512


def _ffn_kernel(a_ref, wm_ref, h_ref, g_ref, wi_ref, wo_ref, o_ref, xn_ref, acc_ref):
    if len(a_ref.shape) == 3:
        a = jnp.concatenate([a_ref[hd] for hd in range(a_ref.shape[0])], axis=1)
    else:
        a = a_ref[...]
    x = h_ref[...] + jnp.dot(a, wm_ref[...], preferred_element_type=F32)
    xn_ref[...] = _rms_rows(x, g_ref[...]).astype(BF16)
    acc_ref[...] = x
    for c0 in range(0, FF, FF_CHUNK):
        c1 = min(c0 + FF_CHUNK, FF)
        xn = xn_ref[...]
        gate = jnp.dot(xn, wi_ref[:, c0:c1], preferred_element_type=F32)
        up = jnp.dot(xn, wi_ref[:, FF + c0:FF + c1], preferred_element_type=F32)
        act = (gate * (1.0 / (1.0 + jnp.exp(-gate))) * up).astype(BF16)
        acc_ref[...] += jnp.dot(act, wo_ref[c0:c1, :], preferred_element_type=F32)
    o_ref[...] = acc_ref[...]


def _ffn(a, wm, h, g, wi, wo, name, out_seq=None):
    b, lp, d = h.shape
    const = lambda *_: (0, 0)
    weights = [
        pl.BlockSpec((d, d), const, pipeline_mode=pl.Buffered(1)),
        pl.BlockSpec((1, d), const),
        pl.BlockSpec((d, 2 * FF), const, pipeline_mode=pl.Buffered(1)),
        pl.BlockSpec((FF, d), const, pipeline_mode=pl.Buffered(1)),
    ]
    if out_seq is None:
        rows = b * lp
        tm = _row_tile(rows, FFN_ROWS)
        grid = (rows // tm,)
        tile = pl.BlockSpec((tm, d), lambda i: (i, 0))
        h = h.reshape(rows, d)
        if a.ndim == 4:
            a = a.reshape(a.shape[0], rows, a.shape[3])
            a_tile = pl.BlockSpec((a.shape[0], tm, a.shape[2]), lambda i: (0, i, 0))
        else:
            a, a_tile = a.reshape(rows, d), tile
        out_shape = jax.ShapeDtypeStruct((rows, d), F32)
        sem = ("arbitrary",)
    else:
        tm = _row_tile(out_seq, FFN_ROWS)
        grid = (b, out_seq // tm)
        tile = a_tile = pl.BlockSpec((None, tm, d), lambda i, j: (i, j, 0))
        if a.ndim == 4:
            a_tile = pl.BlockSpec((a.shape[0], None, tm, a.shape[3]), lambda i, j: (0, i, j, 0))
        out_shape = jax.ShapeDtypeStruct((b, out_seq, d), F32)
        sem = ("arbitrary", "arbitrary")
    out = pl.pallas_call(
        _ffn_kernel,
        grid=grid,
        in_specs=[a_tile, weights[0], tile, weights[1], weights[2], weights[3]],
        out_specs=tile,
        out_shape=out_shape,
        scratch_shapes=[pltpu.VMEM((tm, d), BF16), pltpu.VMEM((tm, d), F32)],
        compiler_params=_params(*sem),
        name=name,
    )(a, wm, h, g, wi, wo)
    return out if out_seq is not None else out.reshape(b, lp, d)


def _split3(x):
    hi = x.astype(BF16)
    r1 = x - hi.astype(F32)
    mid = r1.astype(BF16)
    lo = (r1 - mid.astype(F32)).astype(BF16)
    return hi, mid, lo


def _dot_exact01(m01, x, dims):
    out = None
    for part in _split3(x):
        t = lax.dot_general(m01, part, dims, preferred_element_type=F32)
        out = t if out is None else out + t
    return out


_NN = (((1,), (0,)), ((), ()))
_NT = (((1,), (1,)), ((), ()))
_TN = (((0,), (0,)), ((), ()))


def _gate_prep_kernel(g_ref, row_ref, col_ref, *, seq):
    nx = seq // M_XCHUNK
    er = lax.broadcasted_iota(jnp.int32, (8, LANES), 0)
    el = lax.broadcasted_iota(jnp.int32, (8, LANES), 1)
    pick = jnp.where(el == jnp.where(er < 4, er, er + 4), 1.0, 0.0).astype(BF16)
    for j in range(nx + 1):
        size = M_XCHUNK if j < nx else TAIL
        rows = slice(j * M_XCHUNK, j * M_XCHUNK + size)
        g = g_ref[0, rows, :]
        logf = jnp.minimum(g, 0.0) - jnp.log(1.0 + jnp.exp(-jnp.abs(g)))
        ig = g
        if j == nx:
            real = lax.broadcasted_iota(jnp.int32, (size, LANES), 0) >= M_PAD
            logf = jnp.where(real, logf, 0.0)
            ig = jnp.where(real, g, NEG)
        ti = lax.broadcasted_iota(jnp.int32, (size, size), 0)
        si = lax.broadcasted_iota(jnp.int32, (size, size), 1)
        pre_m = jnp.where(si <= ti, 1.0, 0.0).astype(BF16)
        pre = _dot_exact01(pre_m, logf, _NN)
        suf = pre[size - 1:size, :] - pre + logf
        lane = lax.broadcasted_iota(jnp.int32, (size, LANES), 1)
        pre_dn = pltpu.roll(pre, LANES - M_HEADS, 1)
        suf_dn = pltpu.roll(suf, LANES - M_HEADS, 1)
        row = jnp.where(lane < 4, ig - pre_dn,
                        jnp.where(lane < 8, pre,
                                  jnp.where(lane < 12, ig - suf_dn, suf)))
        row = jnp.where(lane < 16, row, 0.0)
        row_ref[0, rows, :] = row
        col = _dot_exact01(pick, row, _NT)
        if j == nx:
            col_ref[0, j] = jnp.zeros((8, M_XCHUNK), F32)
            col_ref[0, j, :, 0:size] = col
        else:
            col_ref[0, j] = col


def _gate_prep(g, seq, name):
    b, lp, _ = g.shape
    nblk = seq // M_XCHUNK + 1
    return pl.pallas_call(
        functools.partial(_gate_prep_kernel, seq=seq),
        grid=(b,),
        in_specs=[pl.BlockSpec((1, lp, LANES), lambda i: (i, 0, 0))],
        out_specs=[
            pl.BlockSpec((1, lp, LANES), lambda i: (i, 0, 0)),
            pl.BlockSpec((1, nblk, 8, M_XCHUNK), lambda i: (i, 0, 0, 0)),
        ],
        out_shape=[
            jax.ShapeDtypeStruct((b, lp, LANES), F32),
            jax.ShapeDtypeStruct((b, nblk, 8, M_XCHUNK), F32),
        ],
        compiler_params=_params("arbitrary"),
        name=name,
    )(g)


def _lane_pick(x, idx):
    lane = lax.broadcasted_iota(jnp.int32, x.shape, 1)
    return jnp.sum(jnp.where(lane == idx, x, 0.0), axis=1, keepdims=True)


def _chunk_start(c):
    return c * M_XCHUNK if isinstance(c, int) else pl.multiple_of(c * M_XCHUNK, M_XCHUNK)


def _mlstm_kernel(q_ref, k_ref, v_ref, o_ref, gr_ref, gc_ref, gh_ref, out_ref,
                  hf_ref, hb_ref, cn_ref, m_ref, *, seq):
    head = pl.program_id(1)
    nx = seq // M_XCHUNK
    qk_scale = M_DK ** -0.5
    cn_ref[...] = jnp.zeros_like(cn_ref)
    m_ref[...] = jnp.zeros_like(m_ref)

    def step(d, c, size):
        def wide(x, n):
            return x[:, :n] if n < LANES else jnp.concatenate([x] * (n // LANES), axis=1)

        rows = pl.ds(_chunk_start(c), size)
        ti = lax.broadcasted_iota(jnp.int32, (size, size), 0)
        si = lax.broadcasted_iota(jnp.int32, (size, size), 1)
        q = q_ref[rows, :]
        k = k_ref[rows, :]
        v1 = jnp.concatenate([v_ref[rows, :], jnp.ones((size, LANES), BF16)], axis=1)
        beta = jnp.broadcast_to(_lane_pick(gr_ref[0, rows, :], 8 * d + 4 + head), (size, LANES))
        alpha_c = gc_ref[0, c, pl.ds(4 * d + head, 1), :][:, :size]
        m_prev = m_ref[d]
        cn = cn_ref[d]
        keep = (si <= ti) if d == 0 else (si >= ti)
        dmat = jnp.where(keep, wide(beta, size) + alpha_c, -jnp.inf)
        m_inter = beta + m_prev
        m_comb = jnp.maximum(jnp.broadcast_to(jnp.max(dmat, axis=1, keepdims=True), (size, LANES)),
                             m_inter)
        s = lax.dot_general(q, k, _NT, preferred_element_type=F32)
        w = jnp.exp(dmat - wide(m_comb, size)) * s
        scale = jnp.exp(m_inter - m_comb)
        both = (jnp.dot(w.astype(BF16), v1, preferred_element_type=F32)
                + wide(scale, M_DV + LANES)
                * jnp.dot(q, cn.astype(BF16), preferred_element_type=F32)) * qk_scale
        inv = 1.0 / jnp.maximum(jnp.abs(both[:, M_DV:]), jnp.exp(-m_comb))
        hval = both[:, :M_DV] * wide(inv, M_DV)
        if d == 0:
            hf_ref[rows, :] = hval
            tot = beta[size - 1:size]
        else:
            hb_ref[rows, :] = hval
            tot = beta[0:1]
        a_max = jnp.max(alpha_c, axis=1, keepdims=True)
        m_new = jnp.maximum(tot + m_prev, tot + a_max)
        decay = jnp.exp(tot + m_prev - m_new)
        wk = jnp.exp(alpha_c + wide(tot - m_new, size))
        kw = (k.T.astype(F32) * wk).astype(BF16)
        cn_ref[d] = wide(decay, M_DV + LANES) * cn + jnp.dot(kw, v1, preferred_element_type=F32)
        m_ref[d] = m_new

    step(0, nx, TAIL)
    step(1, nx - 1, M_XCHUNK)

    def body(i, carry):
        step(0, i - 1, M_XCHUNK)
        step(1, nx - 1 - i, M_XCHUNK)
        return carry

    lax.fori_loop(1, nx, body, 0, unroll=True)
    step(0, nx - 1, M_XCHUNK)
    step(1, nx, TAIL)

    def finish(c, size):
        rows = pl.ds(_chunk_start(c), size)
        hsum = hf_ref[rows, :] + hb_ref[rows, :]
        og = o_ref[rows, :].astype(F32)
        out_ref[rows, :] = (_rms_rows(hsum, gh_ref[...]) * (1.0 / (1.0 + jnp.exp(-og)))).astype(BF16)

    def finish_body(c, carry):
        finish(c, M_XCHUNK)
        return carry

    lax.fori_loop(0, nx, finish_body, 0)
    finish(nx, TAIL)


def _mlstm(q, k, v, o, grow, gcol, g_head, seq, name):
    _, b, lp, _ = q.shape
    nblk = seq // M_XCHUNK + 1
    per_head = lambda width: pl.BlockSpec((None, None, lp, width), lambda i, h: (h, i, 0, 0))
    return pl.pallas_call(
        functools.partial(_mlstm_kernel, seq=seq),
        grid=(b, M_HEADS),
        in_specs=[
            per_head(M_DK),
            per_head(M_DK),
            per_head(M_DV),
            per_head(M_DV),
            pl.BlockSpec((1, lp, LANES), lambda i, h: (i, 0, 0)),
            pl.BlockSpec((1, nblk, 8, M_XCHUNK), lambda i, h: (i, 0, 0, 0)),
            pl.BlockSpec((1, M_DV), lambda i, h: (0, 0)),
        ],
        out_specs=per_head(M_DV),
        out_shape=jax.ShapeDtypeStruct((M_HEADS, b, lp, M_DV), BF16),
        scratch_shapes=[
            pltpu.VMEM((lp, M_DV), F32),
            pltpu.VMEM((lp, M_DV), F32),
            pltpu.VMEM((2, M_DK, M_DV + LANES), F32),
            pltpu.VMEM((2, 1, LANES), F32),
        ],
        compiler_params=_params("arbitrary", "arbitrary"),
        name=name,
    )(q, k, v, o, grow, gcol, g_head)


ATTN_TK = 2048
ATTN_TQ = 1056


ATTN_MIN_LOG2_SUM = -60.0
ATTN_BOUND_SLACK = 1.02


def _attn_kernel(q_ref, k_ref, v_ref, lam_ref, gk_ref, gs_ref, out_ref, *, seq, tq, lambda_init):
    lp = seq + TAIL
    tk = min(ATTN_TK, seq)
    q = q_ref[0]
    lane = lax.broadcasted_iota(jnp.int32, (tq, LANES), 1)
    zero = jnp.zeros_like(q)
    qs = (jnp.where(lane < D_HD, q, zero), jnp.where(lane >= D_HD, q, zero))
    kt = k_ref[0, seq:lp, :]
    vt = v_ref[0, seq:lp, :]
    tail_ok = lax.broadcasted_iota(jnp.int32, (tq, TAIL), 1) >= M_PAD
    lam_p = lam_ref[...]
    lam = (jnp.exp(jnp.sum(lam_p[0:1] * lam_p[1:2], axis=1, keepdims=True))
           - jnp.exp(jnp.sum(lam_p[2:3] * lam_p[3:4], axis=1, keepdims=True)) + lambda_init)

    def emit(acc1, l1, acc2, l2):
        o = acc1 / l1 - lam * (acc2 / l2)
        out_ref[0] = (_rms_rows(o, gs_ref[...]) * (1.0 - lambda_init)).astype(BF16)

    group = _group_ones(LANES, D_HD)
    ksq = D_HD * jnp.max(gk_ref[...] * gk_ref[...], axis=1, keepdims=True)
    qf = q.astype(F32)
    qsq = jnp.dot((qf * qf).astype(BF16), group, preferred_element_type=F32)
    bound = jnp.sqrt(qsq * ksq) * ATTN_BOUND_SLACK
    shifts = (jnp.max(jnp.where(lane < D_HD, bound, 0.0), axis=1, keepdims=True),
              jnp.max(jnp.where(lane >= D_HD, bound, 0.0), axis=1, keepdims=True))

    fast = []
    for qc, shift in zip(qs, shifts):
        s = jnp.where(tail_ok, lax.dot_general(qc, kt, _NT, preferred_element_type=F32), NEG)
        e = jnp.exp2(s - shift)
        fast += [jnp.sum(e, axis=1, keepdims=True), jnp.dot(e.astype(BF16), vt, preferred_element_type=F32)]

    def fast_body(j, st):
        rows = pl.ds(pl.multiple_of(j * tk, tk), tk)
        kc = k_ref[0, rows, :]
        vc = v_ref[0, rows, :]
        new = []
        for c, (qc, shift) in enumerate(zip(qs, shifts)):
            l, acc = st[2 * c:2 * c + 2]
            e = jnp.exp2(lax.dot_general(qc, kc, _NT, preferred_element_type=F32) - shift)
            new += [l + jnp.sum(e, axis=1, keepdims=True),
                    acc + jnp.dot(e.astype(BF16), vc, preferred_element_type=F32)]
        return tuple(new)

    l1, acc1, l2, acc2 = lax.fori_loop(0, seq // tk, fast_body, tuple(fast), unroll=True)
    emit(acc1, l1, acc2, l2)

    @pl.when(jnp.logical_not(jnp.min(jnp.minimum(l1, l2)) >= 2.0 ** ATTN_MIN_LOG2_SUM))
    def _():
        state = []
        for qc in qs:
            s = jnp.where(tail_ok, lax.dot_general(qc, kt, _NT, preferred_element_type=F32), NEG)
            m = jnp.max(s, axis=1, keepdims=True)
            e = jnp.exp2(s - m)
            state += [m, jnp.sum(e, axis=1, keepdims=True),
                      jnp.dot(e.astype(BF16), vt, preferred_element_type=F32)]

        def body(j, st):
            rows = pl.ds(pl.multiple_of(j * tk, tk), tk)
            kc = k_ref[0, rows, :]
            vc = v_ref[0, rows, :]
            new = []
            for c, qc in enumerate(qs):
                m, l, acc = st[3 * c:3 * c + 3]
                s = lax.dot_general(qc, kc, _NT, preferred_element_type=F32)
                m_new = jnp.maximum(m, jnp.max(s, axis=1, keepdims=True))
                a = jnp.exp2(m - m_new)
                e = jnp.exp2(s - m_new)
                new += [m_new, a * l + jnp.sum(e, axis=1, keepdims=True),
                        a * acc + jnp.dot(e.astype(BF16), vc, preferred_element_type=F32)]
            return tuple(new)

        _, x1, a1, _, x2, a2 = lax.fori_loop(0, seq // tk, body, tuple(state))
        emit(a1, x1, a2, x2)


def _attn(proj, lam, gk, gs, seq, lambda_init, name):
    b, lp, _ = proj.shape
    tq = _row_tile(lp, ATTN_TQ)
    full = lambda i, h, j: (0, 0)
    return pl.pallas_call(
        functools.partial(_attn_kernel, seq=seq, tq=tq, lambda_init=lambda_init),
        grid=(b, D_HEADS, lp // tq),
        in_specs=[
            pl.BlockSpec((1, tq, LANES), lambda i, h, j: (i, j, h)),
            pl.BlockSpec((1, lp, LANES), lambda i, h, j: (i, 0, D_HEADS + h)),
            pl.BlockSpec((1, lp, LANES), lambda i, h, j: (i, 0, 2 * D_HEADS + h)),
            pl.BlockSpec((4, D_HD), full),
            pl.BlockSpec((1, D_HD), full),
            pl.BlockSpec((1, LANES), full),
        ],
        out_specs=pl.BlockSpec((1, tq, LANES), lambda i, h, j: (i, j, h)),
        out_shape=jax.ShapeDtypeStruct((b, lp, D_V_W), BF16),
        compiler_params=_params("arbitrary", "arbitrary", "arbitrary"),
        name=name,
    )(proj, proj, proj, lam, gk, gs)


def _rope_tables(seq, g_q, g_k):
    lp = seq + TAIL
    r = jnp.arange(lp)
    pos = jnp.where(r < seq, r + N_META, jnp.maximum(r - seq - M_PAD, 0)).astype(F32)
    half = ROPE_DIM // 2
    inv = ROPE_THETA ** (-jnp.arange(0, ROPE_DIM, 2, dtype=F32) / ROPE_DIM)
    ang = pos[:, None] * inv[None, :]
    cos, sin = jnp.cos(ang), jnp.sin(ang)
    rest = D_HD - ROPE_DIM

    def fold(g, scale):
        g = g.astype(F32) * scale
        cos_t = jnp.concatenate([cos * g[:half], cos * g[half:ROPE_DIM],
                                 jnp.broadcast_to(g[ROPE_DIM:], (lp, rest))], axis=1)
        sin_dn = jnp.concatenate([-sin * g[half:ROPE_DIM], jnp.zeros((lp, half + rest), F32)], axis=1)
        sin_up = jnp.concatenate([jnp.zeros((lp, half), F32), sin * g[:half],
                                  jnp.zeros((lp, rest), F32)], axis=1)
        return jnp.stack([jnp.tile(t, (1, 2)) for t in (cos_t, sin_dn, sin_up)])

    return jnp.stack([fold(g_q, D_HD ** -0.5 * math.log2(math.e)), fold(g_k, 1.0)])


def _trunk(x, meta_tokens, norm_mix, norm_ffn, m_w_main, m_w_gate, m_b_gate, m_g_head, m_w_out,
           d_w_in, d_g_q, d_g_k, d_lam, d_g_sub, d_w_out, ffn_w_in, ffn_w_out, tag):
    b, seq, d = x.shape
    lp = seq + TAIL
    rows = b * lp
    meta = jnp.broadcast_to(meta_tokens[None].astype(x.dtype), (b, N_META, d))
    h = jnp.concatenate([x, jnp.zeros((b, M_PAD, d), x.dtype), meta], axis=1)
    for i in range(DEPTH):
        j = i // 2
        gmix = norm_mix[i][None]
        hrows = h.reshape(rows, d)
        if i % 2 == 0:
            *qkvo, gates = _m_in(hrows, gmix, m_w_main[j], m_w_gate[j], m_b_gate[j], f"m_in_{tag}{i}")
            grow, gcol = _gate_prep(gates.reshape(b, lp, LANES), seq, f"m_prep_{tag}{i}")
            qkvo = [t.reshape(M_HEADS, b, lp, t.shape[-1]) for t in qkvo]
            mixed = _mlstm(*qkvo, grow, gcol, m_g_head[j][None], seq, f"m_cell_{tag}{i}")
            w_mix_out = m_w_out[j]
        else:
            lambda_init = 0.8 - 0.6 * math.exp(-0.3 * i)
            tables = _rope_tables(seq, d_g_q[j], d_g_k[j])
            proj = _qkv_rope(hrows, gmix, d_w_in[j], tables, lp, f"d_in_{tag}{i}")
            mixed = _attn(proj.reshape(b, lp, 3 * D_QK_W), d_lam[j], d_g_k[j].astype(F32)[None],
                          d_g_sub[j], seq, lambda_init, f"d_attn_{tag}{i}")
            w_mix_out = d_w_out[j]
        h = _ffn(mixed, w_mix_out, h, norm_ffn[i][None], ffn_w_in[i], ffn_w_out[i], f"ffn_{tag}{i}",
                 out_seq=seq if i == DEPTH - 1 else None)
    return h


def kernel(x_prompt, x_sample, meta_tokens, norm_mix, norm_ffn, m_w_in, m_b_gate, m_g_head, m_w_out,
           d_w_in, d_g_q, d_g_k, d_lam_q1, d_lam_k1, d_lam_q2, d_lam_k2, d_g_sub, d_w_out,
           ffn_w_in, ffn_w_out):
    n_gate = 4 * M_HEADS
    m_w_main = m_w_in[:, :, :M_MAIN_W].astype(BF16)
    m_w_gate = jnp.pad(m_w_in[:, :, M_MAIN_W:], ((0, 0), (0, 0), (0, LANES - n_gate))).astype(BF16)
    b_gate = jnp.pad(m_b_gate.astype(F32), ((0, 0), (0, LANES - n_gate)))[:, None, :]
    d_lam = jnp.stack([d_lam_q1, d_lam_k1, d_lam_q2, d_lam_k2], axis=1).astype(F32)
    args = (meta_tokens, norm_mix.astype(F32), norm_ffn.astype(F32), m_w_main, m_w_gate, b_gate,
            m_g_head.astype(F32), m_w_out.astype(BF16), d_w_in.astype(BF16), d_g_q, d_g_k,
            d_lam, d_g_sub.astype(F32)[:, None, :], d_w_out.astype(BF16), ffn_w_in.astype(BF16),
            ffn_w_out.astype(BF16))
    return (_trunk(x_prompt, *args, "p"), _trunk(x_sample, *args, "s"))
```

```python
import functools
import math

import jax
import jax.numpy as jnp
from jax import lax
from jax.experimental import pallas as pl
from jax.experimental.pallas import tpu as pltpu

F32 = jnp.float32
BF16 = jnp.bfloat16

D_MODEL = 1024
DEPTH = 4
N_META = 16
RMS_EPS = 1e-6
NEG = -1e30
M_HEADS = 4
M_DK = 128
M_DV = 256
M_CHUNK = 64
M_XCHUNK = 256
M_PAD = M_CHUNK - N_META
M_QK_W = M_HEADS * M_DK
M_V_W = M_HEADS * M_DV
M_MAIN_W = 2 * M_QK_W + 2 * M_V_W
D_HEADS = 8
D_HD = 64
D_QK_W = D_HEADS * 2 * D_HD
D_V_W = D_HEADS * 2 * D_HD
ROPE_THETA = 500000.0
ROPE_DIM = D_HD // 4
FF = 2816
TAIL = M_CHUNK
LANES = 128
VMEM_LIMIT = 56 * 1024 * 1024


def _params(*sem):
    return pltpu.CompilerParams(dimension_semantics=sem, vmem_limit_bytes=VMEM_LIMIT)


def _row_tile(rows, target):
    best = 16
    for t in range(16, target + 1, 16):
        if rows % t == 0:
            best = t
    return best


def _rms_rows(x, g):
    ms = jnp.mean(x * x, axis=-1, keepdims=True)
    return x * lax.rsqrt(ms + RMS_EPS) * g


MM_COLS = 1024


def _m_in_kernel(x_ref, g_ref, w_ref, wg_ref, bg_ref, q_ref, k_ref, v_ref, o_ref, og_ref):
    xn = _rms_rows(x_ref[...], g_ref[...]).astype(BF16)
    outs = ((q_ref, M_DK), (k_ref, M_DK), (v_ref, M_DV), (o_ref, M_DV))
    col = 0
    for ref, width in outs:
        total = M_HEADS * width
        cols = min(MM_COLS, total)
        for c0 in range(0, total, cols):
            acc = jnp.dot(xn, w_ref[:, col + c0:col + c0 + cols],
                          preferred_element_type=F32).astype(BF16)
            for t in range(cols // width):
                ref[c0 // width + t] = acc[:, t * width:(t + 1) * width]
        col += total
    og_ref[...] = jnp.dot(xn, wg_ref[...], preferred_element_type=F32) + bg_ref[...]


def _m_in(x, g, w, wg, bg, name):
    rows, d = x.shape
    tm = _row_tile(rows, 1056)
    const = lambda i: (0, 0)
    head_major = lambda width: pl.BlockSpec((M_HEADS, tm, width), lambda i: (0, i, 0))
    widths = (M_DK, M_DK, M_DV, M_DV)
    return pl.pallas_call(
        _m_in_kernel,
        grid=(rows // tm,),
        in_specs=[
            pl.BlockSpec((tm, d), lambda i: (i, 0)),
            pl.BlockSpec((1, d), const),
            pl.BlockSpec((d, M_MAIN_W), const, pipeline_mode=pl.Buffered(1)),
            pl.BlockSpec((d, LANES), const, pipeline_mode=pl.Buffered(1)),
            pl.BlockSpec((1, LANES), const),
        ],
        out_specs=[head_major(wd) for wd in widths] + [pl.BlockSpec((tm, LANES), lambda i: (i, 0))],
        out_shape=[jax.ShapeDtypeStruct((M_HEADS, rows, wd), BF16) for wd in widths]
        + [jax.ShapeDtypeStruct((rows, LANES), F32)],
        compiler_params=_params("arbitrary"),
        name=name,
    )(x, g, w, wg, bg)


def _group_ones(width, group):
    gi = lax.broadcasted_iota(jnp.int32, (width, width), 0) // group
    gj = lax.broadcasted_iota(jnp.int32, (width, width), 1) // group
    return jnp.where(gi == gj, 1.0, 0.0).astype(BF16)


def _qkv_rope_kernel(x_ref, g_ref, w_ref, tab_ref, o_ref):
    xn = _rms_rows(x_ref[...], g_ref[...]).astype(BF16)
    wide = 2 * LANES
    group = _group_ones(wide, D_HD)
    half = ROPE_DIM // 2
    for j in range(3):
        acc = jnp.dot(xn, w_ref[:, j * D_QK_W:(j + 1) * D_QK_W], preferred_element_type=F32)
        if j == 2:
            o_ref[:, j * D_QK_W:(j + 1) * D_QK_W] = acc.astype(BF16)
            continue
        for p in range(D_QK_W // wide):
            x2 = acc[:, p * wide:(p + 1) * wide]
            ssq = jnp.dot((x2 * x2).astype(BF16), group, preferred_element_type=F32)
            xs = x2 * lax.rsqrt(ssq * (1.0 / D_HD) + RMS_EPS)
            for hf in range(2):
                xh = xs[:, hf * LANES:(hf + 1) * LANES]
                out = (xh * tab_ref[j, 0] + pltpu.roll(xh, LANES - half, 1) * tab_ref[j, 1]
                       + pltpu.roll(xh, half, 1) * tab_ref[j, 2])
                c0 = j * D_QK_W + p * wide + hf * LANES
                o_ref[:, c0:c0 + LANES] = out.astype(BF16)


def _qkv_rope(x, g, w, tables, lp, name):
    rows, d = x.shape
    n = w.shape[1]
    tm = _row_tile(lp, 1056)
    nb = lp // tm
    return pl.pallas_call(
        _qkv_rope_kernel,
        grid=(rows // tm,),
        in_specs=[
            pl.BlockSpec((tm, d), lambda i: (i, 0)),
            pl.BlockSpec((1, d), lambda i: (0, 0)),
            pl.BlockSpec((d, n), lambda i: (0, 0), pipeline_mode=pl.Buffered(1)),
            pl.BlockSpec((2, 3, tm, LANES), lambda i: (0, 0, i % nb, 0)),
        ],
        out_specs=pl.BlockSpec((tm, n), lambda i: (i, 0)),
        out_shape=jax.ShapeDtypeStruct((rows, n), BF16),
        compiler_params=_params("arbitrary"),
        name=name,
    )(x, g, w, tables)


FF_CHUNK = 512
FFN_ROWS = 512


def _ffn_kernel(a_ref, wm_ref, h_ref, g_ref, wi_ref, wo_ref, o_ref, xn_ref, acc_ref):
    if len(a_ref.shape) == 3:
        a = jnp.concatenate([a_ref[hd] for hd in range(a_ref.shape[0])], axis=1)
    else:
        a = a_ref[...]
    x = h_ref[...] + jnp.dot(a, wm_ref[...], preferred_element_type=F32)
    xn_ref[...] = _rms_rows(x, g_ref[...]).astype(BF16)
    acc_ref[...] = x
    for c0 in range(0, FF, FF_CHUNK):
        c1 = min(c0 + FF_CHUNK, FF)
        xn = xn_ref[...]
        gate = jnp.dot(xn, wi_ref[:, c0:c1], preferred_element_type=F32)
        up = jnp.dot(xn, wi_ref[:, FF + c0:FF + c1], preferred_element_type=F32)
        act = (gate * (1.0 / (1.0 + jnp.exp(-gate))) * up).astype(BF16)
        acc_ref[...] += jnp.dot(act, wo_ref[c0:c1, :], preferred_element_type=F32)
    o_ref[...] = acc_ref[...]


def _ffn(a, wm, h, g, wi, wo, name, out_seq=None):
    b, lp, d = h.shape
    const = lambda *_: (0, 0)
    weights = [
        pl.BlockSpec((d, d), const, pipeline_mode=pl.Buffered(1)),
        pl.BlockSpec((1, d), const),
        pl.BlockSpec((d, 2 * FF), const, pipeline_mode=pl.Buffered(1)),
        pl.BlockSpec((FF, d), const, pipeline_mode=pl.Buffered(1)),
    ]
    if out_seq is None:
        rows = b * lp
        tm = _row_tile(rows, FFN_ROWS)
        grid = (rows // tm,)
        tile = pl.BlockSpec((tm, d), lambda i: (i, 0))
        h = h.reshape(rows, d)
        if a.ndim == 4:
            a = a.reshape(a.shape[0], rows, a.shape[3])
            a_tile = pl.BlockSpec((a.shape[0], tm, a.shape[2]), lambda i: (0, i, 0))
        else:
            a, a_tile = a.reshape(rows, d), tile
        out_shape = jax.ShapeDtypeStruct((rows, d), F32)
        sem = ("arbitrary",)
    else:
        tm = _row_tile(out_seq, FFN_ROWS)
        grid = (b, out_seq // tm)
        tile = a_tile = pl.BlockSpec((None, tm, d), lambda i, j: (i, j, 0))
        if a.ndim == 4:
            a_tile = pl.BlockSpec((a.shape[0], None, tm, a.shape[3]), lambda i, j: (0, i, j, 0))
        out_shape = jax.ShapeDtypeStruct((b, out_seq, d), F32)
        sem = ("arbitrary", "arbitrary")
    out = pl.pallas_call(
        _ffn_kernel,
        grid=grid,
        in_specs=[a_tile, weights[0], tile, weights[1], weights[2], weights[3]],
        out_specs=tile,
        out_shape=out_shape,
        scratch_shapes=[pltpu.VMEM((tm, d), BF16), pltpu.VMEM((tm, d), F32)],
        compiler_params=_params(*sem),
        name=name,
    )(a, wm, h, g, wi, wo)
    return out if out_seq is not None else out.reshape(b, lp, d)


def _split3(x):
    hi = x.astype(BF16)
    r1 = x - hi.astype(F32)
    mid = r1.astype(BF16)
    lo = (r1 - mid.astype(F32)).astype(BF16)
    return hi, mid, lo


def _dot_exact01(m01, x, dims):
    out = None
    for part in _split3(x):
        t = lax.dot_general(m01, part, dims, preferred_element_type=F32)
        out = t if out is None else out + t
    return out


_NN = (((1,), (0,)), ((), ()))
_NT = (((1,), (1,)), ((), ()))


def _gate_prep_kernel(g_ref, row_ref, col_ref, *, seq):
    nx = seq // M_XCHUNK
    er = lax.broadcasted_iota(jnp.int32, (8, LANES), 0)
    el = lax.broadcasted_iota(jnp.int32, (8, LANES), 1)
    pick = jnp.where(el == jnp.where(er < 4, er, er + 4), 1.0, 0.0).astype(BF16)
    for j in range(nx + 1):
        size = M_XCHUNK if j < nx else TAIL
        rows = slice(j * M_XCHUNK, j * M_XCHUNK + size)
        g = g_ref[0, rows, :]
        logf = jnp.minimum(g, 0.0) - jnp.log(1.0 + jnp.exp(-jnp.abs(g)))
        ig = g
        if j == nx:
            real = lax.broadcasted_iota(jnp.int32, (size, LANES), 0) >= M_PAD
            logf = jnp.where(real, logf, 0.0)
            ig = jnp.where(real, g, NEG)
        ti = lax.broadcasted_iota(jnp.int32, (size, size), 0)
        si = lax.broadcasted_iota(jnp.int32, (size, size), 1)
        pre_m = jnp.where(si <= ti, 1.0, 0.0).astype(BF16)
        pre = _dot_exact01(pre_m, logf, _NN)
        suf = pre[size - 1:size, :] - pre + logf
        lane = lax.broadcasted_iota(jnp.int32, (size, LANES), 1)
        pre_dn = pltpu.roll(pre, LANES - M_HEADS, 1)
        suf_dn = pltpu.roll(suf, LANES - M_HEADS, 1)
        row = jnp.where(lane < 4, ig - pre_dn,
                        jnp.where(lane < 8, pre,
                                  jnp.where(lane < 12, ig - suf_dn, suf)))
        row = jnp.where(lane < 16, row, 0.0)
        row_ref[0, rows, :] = row
        col = _dot_exact01(pick, row, _NT)
        if j == nx:
            col_ref[0, j] = jnp.zeros((8, M_XCHUNK), F32)
            col_ref[0, j, :, 0:size] = col
        else:
            col_ref[0, j] = col


def _gate_prep(g, seq, name):
    b, lp, _ = g.shape
    nblk = seq // M_XCHUNK + 1
    return pl.pallas_call(
        functools.partial(_gate_prep_kernel, seq=seq),
        grid=(b,),
        in_specs=[pl.BlockSpec((1, lp, LANES), lambda i: (i, 0, 0))],
        out_specs=[
            pl.BlockSpec((1, lp, LANES), lambda i: (i, 0, 0)),
            pl.BlockSpec((1, nblk, 8, M_XCHUNK), lambda i: (i, 0, 0, 0)),
        ],
        out_shape=[
            jax.ShapeDtypeStruct((b, lp, LANES), F32),
            jax.ShapeDtypeStruct((b, nblk, 8, M_XCHUNK), F32),
        ],
        compiler_params=_params("arbitrary"),
        name=name,
    )(g)


def _lane_pick(x, idx):
    lane = lax.broadcasted_iota(jnp.int32, x.shape, 1)
    return jnp.sum(jnp.where(lane == idx, x, 0.0), axis=1, keepdims=True)


def _chunk_start(c):
    return c * M_XCHUNK if isinstance(c, int) else pl.multiple_of(c * M_XCHUNK, M_XCHUNK)


def _mlstm_kernel(q_ref, k_ref, v_ref, o_ref, gr_ref, gc_ref, gh_ref, out_ref,
                  hf_ref, hb_ref, cn_ref, m_ref, *, seq):
    head = pl.program_id(1)
    nx = seq // M_XCHUNK
    qk_scale = M_DK ** -0.5
    cn_ref[...] = jnp.zeros_like(cn_ref)
    m_ref[...] = jnp.zeros_like(m_ref)

    def step(d, c, size):
        def wide(x, n):
            return x[:, :n] if n < LANES else jnp.concatenate([x] * (n // LANES), axis=1)

        rows = pl.ds(_chunk_start(c), size)
        ti = lax.broadcasted_iota(jnp.int32, (size, size), 0)
        si = lax.broadcasted_iota(jnp.int32, (size, size), 1)
        q = q_ref[rows, :]
        k = k_ref[rows, :]
        v1 = jnp.concatenate([v_ref[rows, :], jnp.ones((size, LANES), BF16)], axis=1)
        beta = jnp.broadcast_to(_lane_pick(gr_ref[0, rows, :], 8 * d + 4 + head), (size, LANES))
        alpha_c = gc_ref[0, c, pl.ds(4 * d + head, 1), :][:, :size]
        m_prev = m_ref[d]
        cn = cn_ref[d]
        keep = (si <= ti) if d == 0 else (si >= ti)
        dmat = jnp.where(keep, wide(beta, size) + alpha_c, -jnp.inf)
        m_inter = beta + m_prev
        m_comb = jnp.maximum(jnp.broadcast_to(jnp.max(dmat, axis=1, keepdims=True), (size, LANES)),
                             m_inter)
        s = lax.dot_general(q, k, _NT, preferred_element_type=F32)
        w = jnp.exp(dmat - wide(m_comb, size)) * s
        scale = jnp.exp(m_inter - m_comb)
        both = (jnp.dot(w.astype(BF16), v1, preferred_element_type=F32)
                + wide(scale, M_DV + LANES)
                * jnp.dot(q, cn.astype(BF16), preferred_element_type=F32)) * qk_scale
        inv = 1.0 / jnp.maximum(jnp.abs(both[:, M_DV:]), jnp.exp(-m_comb))
        hval = both[:, :M_DV] * wide(inv, M_DV)
        if d == 0:
            hf_ref[rows, :] = hval
            tot = beta[size - 1:size]
        else:
            hb_ref[rows, :] = hval
            tot = beta[0:1]
        a_max = jnp.max(alpha_c, axis=1, keepdims=True)
        m_new = jnp.maximum(tot + m_prev, tot + a_max)
        decay = jnp.exp(tot + m_prev - m_new)
        wk = jnp.exp(alpha_c + wide(tot - m_new, size))
        kw = (k.T.astype(F32) * wk).astype(BF16)
        cn_ref[d] = wide(decay, M_DV + LANES) * cn + jnp.dot(kw, v1, preferred_element_type=F32)
        m_ref[d] = m_new

    step(0, nx, TAIL)
    step(1, nx - 1, M_XCHUNK)

    def body(i, carry):
        step(0, i - 1, M_XCHUNK)
        step(1, nx - 1 - i, M_XCHUNK)
        return carry

    lax.fori_loop(1, nx, body, 0, unroll=True)
    step(0, nx - 1, M_XCHUNK)
    step(1, nx, TAIL)

    def finish(c, size):
        rows = pl.ds(_chunk_start(c), size)
        hsum = hf_ref[rows, :] + hb_ref[rows, :]
        og = o_ref[rows, :].astype(F32)
        out_ref[rows, :] = (_rms_rows(hsum, gh_ref[...]) * (1.0 / (1.0 + jnp.exp(-og)))).astype(BF16)

    def finish_body(c, carry):
        finish(c, M_XCHUNK)
        return carry

    lax.fori_loop(0, nx, finish_body, 0)
    finish(nx, TAIL)


def _mlstm(q, k, v, o, grow, gcol, g_head, seq, name):
    _, b, lp, _ = q.shape
    nblk = seq // M_XCHUNK + 1
    per_head = lambda width: pl.BlockSpec((None, None, lp, width), lambda i, h: (h, i, 0, 0))
    return pl.pallas_call(
        functools.partial(_mlstm_kernel, seq=seq),
        grid=(b, M_HEADS),
        in_specs=[
            per_head(M_DK),
            per_head(M_DK),
            per_head(M_DV),
            per_head(M_DV),
            pl.BlockSpec((1, lp, LANES), lambda i, h: (i, 0, 0)),
            pl.BlockSpec((1, nblk, 8, M_XCHUNK), lambda i, h: (i, 0, 0, 0)),
            pl.BlockSpec((1, M_DV), lambda i, h: (0, 0)),
        ],
        out_specs=per_head(M_DV),
        out_shape=jax.ShapeDtypeStruct((M_HEADS, b, lp, M_DV), BF16),
        scratch_shapes=[
            pltpu.VMEM((lp, M_DV), F32),
            pltpu.VMEM((lp, M_DV), F32),
            pltpu.VMEM((2, M_DK, M_DV + LANES), F32),
            pltpu.VMEM((2, 1, LANES), F32),
        ],
        compiler_params=_params("arbitrary", "arbitrary"),
        name=name,
    )(q, k, v, o, grow, gcol, g_head)


ATTN_TK = 2048
ATTN_TQ = 1056


ATTN_MIN_LOG2_SUM = -60.0
ATTN_BOUND_SLACK = 1.02


def _attn_kernel(q_ref, k_ref, v_ref, lam_ref, gk_ref, gs_ref, out_ref, *, seq, tq, lambda_init):
    lp = seq + TAIL
    tk = min(ATTN_TK, seq)
    q = q_ref[0]
    lane = lax.broadcasted_iota(jnp.int32, (tq, LANES), 1)
    zero = jnp.zeros_like(q)
    qs = (jnp.where(lane < D_HD, q, zero), jnp.where(lane >= D_HD, q, zero))
    kt = k_ref[0, seq:lp, :]
    vt = v_ref[0, seq:lp, :]
    tail_ok = lax.broadcasted_iota(jnp.int32, (tq, TAIL), 1) >= M_PAD
    lam_p = lam_ref[...]
    lam = (jnp.exp(jnp.sum(lam_p[0:1] * lam_p[1:2], axis=1, keepdims=True))
           - jnp.exp(jnp.sum(lam_p[2:3] * lam_p[3:4], axis=1, keepdims=True)) + lambda_init)

    def emit(acc1, l1, acc2, l2):
        o = acc1 / l1 - lam * (acc2 / l2)
        out_ref[0] = (_rms_rows(o, gs_ref[...]) * (1.0 - lambda_init)).astype(BF16)

    group = _group_ones(LANES, D_HD)
    ksq = D_HD * jnp.max(gk_ref[...] * gk_ref[...], axis=1, keepdims=True)
    qf = q.astype(F32)
    qsq = jnp.dot((qf * qf).astype(BF16), group, preferred_element_type=F32)
    bound = jnp.sqrt(qsq * ksq) * ATTN_BOUND_SLACK
    shifts = (jnp.max(jnp.where(lane < D_HD, bound, 0.0), axis=1, keepdims=True),
              jnp.max(jnp.where(lane >= D_HD, bound, 0.0), axis=1, keepdims=True))

    fast = []
    for qc, shift in zip(qs, shifts):
        s = jnp.where(tail_ok, lax.dot_general(qc, kt, _NT, preferred_element_type=F32), NEG)
        e = jnp.exp2(s - shift)
        fast += [jnp.sum(e, axis=1, keepdims=True), jnp.dot(e.astype(BF16), vt, preferred_element_type=F32)]

    def fast_body(j, st):
        rows = pl.ds(pl.multiple_of(j * tk, tk), tk)
        kc = k_ref[0, rows, :]
        vc = v_ref[0, rows, :]
        new = []
        for c, (qc, shift) in enumerate(zip(qs, shifts)):
            l, acc = st[2 * c:2 * c + 2]
            e = jnp.exp2(lax.dot_general(qc, kc, _NT, preferred_element_type=F32) - shift)
            new += [l + jnp.sum(e, axis=1, keepdims=True),
                    acc + jnp.dot(e.astype(BF16), vc, preferred_element_type=F32)]
        return tuple(new)

    l1, acc1, l2, acc2 = lax.fori_loop(0, seq // tk, fast_body, tuple(fast), unroll=True)
    emit(acc1, l1, acc2, l2)

    @pl.when(jnp.logical_not(jnp.min(jnp.minimum(l1, l2)) >= 2.0 ** ATTN_MIN_LOG2_SUM))
    def _():
        state = []
        for qc in qs:
            s = jnp.where(tail_ok, lax.dot_general(qc, kt, _NT, preferred_element_type=F32), NEG)
            m = jnp.max(s, axis=1, keepdims=True)
            e = jnp.exp2(s - m)
            state += [m, jnp.sum(e, axis=1, keepdims=True),
                      jnp.dot(e.astype(BF16), vt, preferred_element_type=F32)]

        def body(j, st):
            rows = pl.ds(pl.multiple_of(j * tk, tk), tk)
            kc = k_ref[0, rows, :]
            vc = v_ref[0, rows, :]
            new = []
            for c, qc in enumerate(qs):
                m, l, acc = st[3 * c:3 * c + 3]
                s = lax.dot_general(qc, kc, _NT, preferred_element_type=F32)
                m_new = jnp.maximum(m, jnp.max(s, axis=1, keepdims=True))
                a = jnp.exp2(m - m_new)
                e = jnp.exp2(s - m_new)
                new += [m_new, a * l + jnp.sum(e, axis=1, keepdims=True),
                        a * acc + jnp.dot(e.astype(BF16), vc, preferred_element_type=F32)]
            return tuple(new)

        _, x1, a1, _, x2, a2 = lax.fori_loop(0, seq // tk, body, tuple(state))
        emit(a1, x1, a2, x2)


def _attn(proj, lam, gk, gs, seq, lambda_init, name):
    b, lp, _ = proj.shape
    tq = _row_tile(lp, ATTN_TQ)
    full = lambda i, h, j: (0, 0)
    return pl.pallas_call(
        functools.partial(_attn_kernel, seq=seq, tq=tq, lambda_init=lambda_init),
        grid=(b, D_HEADS, lp // tq),
        in_specs=[
            pl.BlockSpec((1, tq, LANES), lambda i, h, j: (i, j, h)),
            pl.BlockSpec((1, lp, LANES), lambda i, h, j: (i, 0, D_HEADS + h)),
            pl.BlockSpec((1, lp, LANES), lambda i, h, j: (i, 0, 2 * D_HEADS + h)),
            pl.BlockSpec((4, D_HD), full),
            pl.BlockSpec((1, D_HD), full),
            pl.BlockSpec((1, LANES), full),
        ],
        out_specs=pl.BlockSpec((1, tq, LANES), lambda i, h, j: (i, j, h)),
        out_shape=jax.ShapeDtypeStruct((b, lp, D_V_W), BF16),
        compiler_params=_params("arbitrary", "arbitrary", "arbitrary"),
        name=name,
    )(proj, proj, proj, lam, gk, gs)


def _rope_tables(seq, g_q, g_k):
    lp = seq + TAIL
    r = jnp.arange(lp)
    pos = jnp.where(r < seq, r + N_META, jnp.maximum(r - seq - M_PAD, 0)).astype(F32)
    half = ROPE_DIM // 2
    inv = ROPE_THETA ** (-jnp.arange(0, ROPE_DIM, 2, dtype=F32) / ROPE_DIM)
    ang = pos[:, None] * inv[None, :]
    cos, sin = jnp.cos(ang), jnp.sin(ang)
    rest = D_HD - ROPE_DIM

    def fold(g, scale):
        g = g.astype(F32) * scale
        cos_t = jnp.concatenate([cos * g[:half], cos * g[half:ROPE_DIM],
                                 jnp.broadcast_to(g[ROPE_DIM:], (lp, rest))], axis=1)
        sin_dn = jnp.concatenate([-sin * g[half:ROPE_DIM], jnp.zeros((lp, half + rest), F32)], axis=1)
        sin_up = jnp.concatenate([jnp.zeros((lp, half), F32), sin * g[:half],
                                  jnp.zeros((lp, rest), F32)], axis=1)
        return jnp.stack([jnp.tile(t, (1, 2)) for t in (cos_t, sin_dn, sin_up)])

    return jnp.stack([fold(g_q, D_HD ** -0.5 * math.log2(math.e)), fold(g_k, 1.0)])


def _trunk(x, meta_tokens, norm_mix, norm_ffn, m_w_main, m_w_gate, m_b_gate, m_g_head, m_w_out,
           d_w_in, d_g_q, d_g_k, d_lam, d_g_sub, d_w_out, ffn_w_in, ffn_w_out, tag):
    b, seq, d = x.shape
    lp = seq + TAIL
    rows = b * lp
    meta = jnp.broadcast_to(meta_tokens[None].astype(x.dtype), (b, N_META, d))
    h = jnp.concatenate([x, jnp.zeros((b, M_PAD, d), x.dtype), meta], axis=1)
    for i in range(DEPTH):
        j = i // 2
        gmix = norm_mix[i][None]
        hrows = h.reshape(rows, d)
        if i % 2 == 0:
            *qkvo, gates = _m_in(hrows, gmix, m_w_main[j], m_w_gate[j], m_b_gate[j], f"m_in_{tag}{i}")
            grow, gcol = _gate_prep(gates.reshape(b, lp, LANES), seq, f"m_prep_{tag}{i}")
            qkvo = [t.reshape(M_HEADS, b, lp, t.shape[-1]) for t in qkvo]
            mixed = _mlstm(*qkvo, grow, gcol, m_g_head[j][None], seq, f"m_cell_{tag}{i}")
            w_mix_out = m_w_out[j]
        else:
            lambda_init = 0.8 - 0.6 * math.exp(-0.3 * i)
            tables = _rope_tables(seq, d_g_q[j], d_g_k[j])
            proj = _qkv_rope(hrows, gmix, d_w_in[j], tables, lp, f"d_in_{tag}{i}")
            mixed = _attn(proj.reshape(b, lp, 3 * D_QK_W), d_lam[j], d_g_k[j].astype(F32)[None],
                          d_g_sub[j], seq, lambda_init, f"d_attn_{tag}{i}")
            w_mix_out = d_w_out[j]
        h = _ffn(mixed, w_mix_out, h, norm_ffn[i][None], ffn_w_in[i], ffn_w_out[i], f"ffn_{tag}{i}",
                 out_seq=seq if i == DEPTH - 1 else None)
    return h


def kernel(x_prompt, x_sample, meta_tokens, norm_mix, norm_ffn, m_w_in, m_b_gate, m_g_head, m_w_out,
           d_w_in, d_g_q, d_g_k, d_lam_q1, d_lam_k1, d_lam_q2, d_lam_k2, d_g_sub, d_w_out,
           ffn_w_in, ffn_w_out):
    n_gate = 4 * M_HEADS
    m_w_main = m_w_in[:, :, :M_MAIN_W].astype(BF16)
    m_w_gate = jnp.pad(m_w_in[:, :, M_MAIN_W:], ((0, 0), (0, 0), (0, LANES - n_gate))).astype(BF16)
    b_gate = jnp.pad(m_b_gate.astype(F32), ((0, 0), (0, LANES - n_gate)))[:, None, :]
    d_lam = jnp.stack([d_lam_q1, d_lam_k1, d_lam_q2, d_lam_k2], axis=1).astype(F32)
    args = (meta_tokens, norm_mix.astype(F32), norm_ffn.astype(F32), m_w_main, m_w_gate, b_gate,
            m_g_head.astype(F32), m_w_out.astype(BF16), d_w_in.astype(BF16), d_g_q, d_g_k,
            d_lam, d_g_sub.astype(F32)[:, None, :], d_w_out.astype(BF16), ffn_w_in.astype(BF16),
            ffn_w_out.astype(BF16))
    return (_trunk(x_prompt, *args, "p"), _trunk(x_sample, *args, "s"))
```

```python
import functools
import math

import jax
import jax.numpy as jnp
from jax import lax
from jax.experimental import pallas as pl
from jax.experimental.pallas import tpu as pltpu

F32 = jnp.float32
BF16 = jnp.bfloat16

D_MODEL = 1024
DEPTH = 4
N_META = 16
RMS_EPS = 1e-6
NEG = -1e30
M_HEADS = 4
M_DK = 128
M_DV = 256
M_CHUNK = 64
M_XCHUNK = 256
M_PAD = M_CHUNK - N_META
M_QK_W = M_HEADS * M_DK
M_V_W = M_HEADS * M_DV
M_MAIN_W = 2 * M_QK_W + 2 * M_V_W
D_HEADS = 8
D_HD = 64
D_QK_W = D_HEADS * 2 * D_HD
D_V_W = D_HEADS * 2 * D_HD
ROPE_THETA = 500000.0
ROPE_DIM = D_HD // 4
FF = 2816
TAIL = M_CHUNK
LANES = 128
VMEM_LIMIT = 56 * 1024 * 1024


def _params(*sem):
    return pltpu.CompilerParams(dimension_semantics=sem, vmem_limit_bytes=VMEM_LIMIT)


def _row_tile(rows, target):
    best = 16
    for t in range(16, target + 1, 16):
        if rows % t == 0:
            best = t
    return best


def _rms_rows(x, g):
    ms = jnp.mean(x * x, axis=-1, keepdims=True)
    return x * lax.rsqrt(ms + RMS_EPS) * g


MM_COLS = 1024


def _m_in_kernel(x_ref, g_ref, w_ref, wg_ref, bg_ref, q_ref, k_ref, v_ref, o_ref, og_ref):
    xn = _rms_rows(x_ref[...], g_ref[...]).astype(BF16)
    outs = ((q_ref, M_DK), (k_ref, M_DK), (v_ref, M_DV), (o_ref, M_DV))
    col = 0
    for ref, width in outs:
        total = M_HEADS * width
        cols = min(MM_COLS, total)
        for c0 in range(0, total, cols):
            acc = jnp.dot(xn, w_ref[:, col + c0:col + c0 + cols],
                          preferred_element_type=F32).astype(BF16)
            for t in range(cols // width):
                ref[c0 // width + t] = acc[:, t * width:(t + 1) * width]
        col += total
    og_ref[...] = jnp.dot(xn, wg_ref[...], preferred_element_type=F32) + bg_ref[...]


def _m_in(x, g, w, wg, bg, name):
    rows, d = x.shape
    tm = _row_tile(rows, 1056)
    const = lambda i: (0, 0)
    head_major = lambda width: pl.BlockSpec((M_HEADS, tm, width), lambda i: (0, i, 0))
    widths = (M_DK, M_DK, M_DV, M_DV)
    return pl.pallas_call(
        _m_in_kernel,
        grid=(rows // tm,),
        in_specs=[
            pl.BlockSpec((tm, d), lambda i: (i, 0)),
            pl.BlockSpec((1, d), const),
            pl.BlockSpec((d, M_MAIN_W), const, pipeline_mode=pl.Buffered(1)),
            pl.BlockSpec((d, LANES), const, pipeline_mode=pl.Buffered(1)),
            pl.BlockSpec((1, LANES), const),
        ],
        out_specs=[head_major(wd) for wd in widths] + [pl.BlockSpec((tm, LANES), lambda i: (i, 0))],
        out_shape=[jax.ShapeDtypeStruct((M_HEADS, rows, wd), BF16) for wd in widths]
        + [jax.ShapeDtypeStruct((rows, LANES), F32)],
        compiler_params=_params("arbitrary"),
        name=name,
    )(x, g, w, wg, bg)


def _group_ones(width, group):
    gi = lax.broadcasted_iota(jnp.int32, (width, width), 0) // group
    gj = lax.broadcasted_iota(jnp.int32, (width, width), 1) // group
    return jnp.where(gi == gj, 1.0, 0.0).astype(BF16)


def _qkv_rope_kernel(x_ref, g_ref, w_ref, tab_ref, o_ref):
    xn = _rms_rows(x_ref[...], g_ref[...]).astype(BF16)
    wide = 2 * LANES
    group = _group_ones(wide, D_HD)
    half = ROPE_DIM // 2
    for j in range(3):
        acc = jnp.dot(xn, w_ref[:, j * D_QK_W:(j + 1) * D_QK_W], preferred_element_type=F32)
        if j == 2:
            o_ref[:, j * D_QK_W:(j + 1) * D_QK_W] = acc.astype(BF16)
            continue
        for p in range(D_QK_W // wide):
            x2 = acc[:, p * wide:(p + 1) * wide]
            ssq = jnp.dot((x2 * x2).astype(BF16), group, preferred_element_type=F32)
            xs = x2 * lax.rsqrt(ssq * (1.0 / D_HD) + RMS_EPS)
            for hf in range(2):
                xh = xs[:, hf * LANES:(hf + 1) * LANES]
                out = (xh * tab_ref[j, 0] + pltpu.roll(xh, LANES - half, 1) * tab_ref[j, 1]
                       + pltpu.roll(xh, half, 1) * tab_ref[j, 2])
                c0 = j * D_QK_W + p * wide + hf * LANES
                o_ref[:, c0:c0 + LANES] = out.astype(BF16)


def _qkv_rope(x, g, w, tables, lp, name):
    rows, d = x.shape
    n = w.shape[1]
    tm = _row_tile(lp, 1056)
    nb = lp // tm
    return pl.pallas_call(
        _qkv_rope_kernel,
        grid=(rows // tm,),
        in_specs=[
            pl.BlockSpec((tm, d), lambda i: (i, 0)),
            pl.BlockSpec((1, d), lambda i: (0, 0)),
            pl.BlockSpec((d, n), lambda i: (0, 0), pipeline_mode=pl.Buffered(1)),
            pl.BlockSpec((2, 3, tm, LANES), lambda i: (0, 0, i % nb, 0)),
        ],
        out_specs=pl.BlockSpec((tm, n), lambda i: (i, 0)),
        out_shape=jax.ShapeDtypeStruct((rows, n), BF16),
        compiler_params=_params("arbitrary"),
        name=name,
    )(x, g, w, tables)


FF_CHUNK = 512
FFN_ROWS = 512


def _ffn_kernel(a_ref, wm_ref, h_ref, g_ref, wi_ref, wo_ref, o_ref, xn_ref, acc_ref):
    if len(a_ref.shape) == 3:
        a = jnp.concatenate([a_ref[hd] for hd in range(a_ref.shape[0])], axis=1)
    else:
        a = a_ref[...]
    x = h_ref[...] + jnp.dot(a, wm_ref[...], preferred_element_type=F32)
    xn_ref[...] = _rms_rows(x, g_ref[...]).astype(BF16)
    acc_ref[...] = x
    for c0 in range(0, FF, FF_CHUNK):
        c1 = min(c0 + FF_CHUNK, FF)
        xn = xn_ref[...]
        gate = jnp.dot(xn, wi_ref[:, c0:c1], preferred_element_type=F32)
        up = jnp.dot(xn, wi_ref[:, FF + c0:FF + c1], preferred_element_type=F32)
        act = (gate * (1.0 / (1.0 + jnp.exp(-gate))) * up).astype(BF16)
        acc_ref[...] += jnp.dot(act, wo_ref[c0:c1, :], preferred_element_type=F32)
    o_ref[...] = acc_ref[...]


def _ffn(a, wm, h, g, wi, wo, name, out_seq=None):
    b, lp, d = h.shape
    const = lambda *_: (0, 0)
    weights = [
        pl.BlockSpec((d, d), const, pipeline_mode=pl.Buffered(1)),
        pl.BlockSpec((1, d), const),
        pl.BlockSpec((d, 2 * FF), const, pipeline_mode=pl.Buffered(1)),
        pl.BlockSpec((FF, d), const, pipeline_mode=pl.Buffered(1)),
    ]
    if out_seq is None:
        rows = b * lp
        tm = _row_tile(rows, FFN_ROWS)
        grid = (rows // tm,)
        tile = pl.BlockSpec((tm, d), lambda i: (i, 0))
        h = h.reshape(rows, d)
        if a.ndim == 4:
            a = a.reshape(a.shape[0], rows, a.shape[3])
            a_tile = pl.BlockSpec((a.shape[0], tm, a.shape[2]), lambda i: (0, i, 0))
        else:
            a, a_tile = a.reshape(rows, d), tile
        out_shape = jax.ShapeDtypeStruct((rows, d), F32)
        sem = ("arbitrary",)
    else:
        tm = _row_tile(out_seq, FFN_ROWS)
        grid = (b, out_seq // tm)
        tile = a_tile = pl.BlockSpec((None, tm, d), lambda i, j: (i, j, 0))
        if a.ndim == 4:
            a_tile = pl.BlockSpec((a.shape[0], None, tm, a.shape[3]), lambda i, j: (0, i, j, 0))
        out_shape = jax.ShapeDtypeStruct((b, out_seq, d), F32)
        sem = ("arbitrary", "arbitrary")
    out = pl.pallas_call(
        _ffn_kernel,
        grid=grid,
        in_specs=[a_tile, weights[0], tile, weights[1], weights[2], weights[3]],
        out_specs=tile,
        out_shape=out_shape,
        scratch_shapes=[pltpu.VMEM((tm, d), BF16), pltpu.VMEM((tm, d), F32)],
        compiler_params=_params(*sem),
        name=name,
    )(a, wm, h, g, wi, wo)
    return out if out_seq is not None else out.reshape(b, lp, d)


def _split3(x):
    hi = x.astype(BF16)
    r1 = x - hi.astype(F32)
    mid = r1.astype(BF16)
    lo = (r1 - mid.astype(F32)).astype(BF16)
    return hi, mid, lo


def _dot_exact01(m01, x, dims):
    out = None
    for part in _split3(x):
        t = lax.dot_general(m01, part, dims, preferred_element_type=F32)
        out = t if out is None else out + t
    return out


_NN = (((1,), (0,)), ((), ()))
_NT = (((1,), (1,)), ((), ()))


def _gate_prep_kernel(g_ref, row_ref, col_ref, *, seq):
    nx = seq // M_XCHUNK
    er = lax.broadcasted_iota(jnp.int32, (8, LANES), 0)
    el = lax.broadcasted_iota(jnp.int32, (8, LANES), 1)
    pick = jnp.where(el == jnp.where(er < 4, er, er + 4), 1.0, 0.0).astype(BF16)
    for j in range(nx + 1):
        size = M_XCHUNK if j < nx else TAIL
        rows = slice(j * M_XCHUNK, j * M_XCHUNK + size)
        g = g_ref[0, rows, :]
        logf = jnp.minimum(g, 0.0) - jnp.log(1.0 + jnp.exp(-jnp.abs(g)))
        ig = g
        if j == nx:
            real = lax.broadcasted_iota(jnp.int32, (size, LANES), 0) >= M_PAD
            logf = jnp.where(real, logf, 0.0)
            ig = jnp.where(real, g, NEG)
        ti = lax.broadcasted_iota(jnp.int32, (size, size), 0)
        si = lax.broadcasted_iota(jnp.int32, (size, size), 1)
        pre_m = jnp.where(si <= ti, 1.0, 0.0).astype(BF16)
        pre = _dot_exact01(pre_m, logf, _NN)
        suf = pre[size - 1:size, :] - pre + logf
        lane = lax.broadcasted_iota(jnp.int32, (size, LANES), 1)
        pre_dn = pltpu.roll(pre, LANES - M_HEADS, 1)
        suf_dn = pltpu.roll(suf, LANES - M_HEADS, 1)
        row = jnp.where(lane < 4, ig - pre_dn,
                        jnp.where(lane < 8, pre,
                                  jnp.where(lane < 12, ig - suf_dn, suf)))
        row = jnp.where(lane < 16, row, 0.0)
        row_ref[0, rows, :] = row
        col = _dot_exact01(pick, row, _NT)
        if j == nx:
            col_ref[0, j] = jnp.zeros((8, M_XCHUNK), F32)
            col_ref[0, j, :, 0:size] = col
        else:
            col_ref[0, j] = col


def _gate_prep(g, seq, name):
    b, lp, _ = g.shape
    nblk = seq // M_XCHUNK + 1
    return pl.pallas_call(
        functools.partial(_gate_prep_kernel, seq=seq),
        grid=(b,),
        in_specs=[pl.BlockSpec((1, lp, LANES), lambda i: (i, 0, 0))],
        out_specs=[
            pl.BlockSpec((1, lp, LANES), lambda i: (i, 0, 0)),
            pl.BlockSpec((1, nblk, 8, M_XCHUNK), lambda i: (i, 0, 0, 0)),
        ],
        out_shape=[
            jax.ShapeDtypeStruct((b, lp, LANES), F32),
            jax.ShapeDtypeStruct((b, nblk, 8, M_XCHUNK), F32),
        ],
        compiler_params=_params("arbitrary"),
        name=name,
    )(g)


def _lane_pick(x, idx):
    lane = lax.broadcasted_iota(jnp.int32, x.shape, 1)
    return jnp.sum(jnp.where(lane == idx, x, 0.0), axis=1, keepdims=True)


def _chunk_start(c):
    return c * M_XCHUNK if isinstance(c, int) else pl.multiple_of(c * M_XCHUNK, M_XCHUNK)


def _mlstm_kernel(q_ref, k_ref, v_ref, o_ref, gr_ref, gc_ref, gh_ref, out_ref,
                  hf_ref, hb_ref, cn_ref, m_ref, *, seq):
    head = pl.program_id(1)
    nx = seq // M_XCHUNK
    qk_scale = M_DK ** -0.5
    cn_ref[...] = jnp.zeros_like(cn_ref)
    m_ref[...] = jnp.zeros_like(m_ref)

    def step(d, c, size):
        def wide(x, n):
            return x[:, :n] if n < LANES else jnp.concatenate([x] * (n // LANES), axis=1)

        rows = pl.ds(_chunk_start(c), size)
        ti = lax.broadcasted_iota(jnp.int32, (size, size), 0)
        si = lax.broadcasted_iota(jnp.int32, (size, size), 1)
        q = q_ref[rows, :]
        k = k_ref[rows, :]
        v1 = jnp.concatenate([v_ref[rows, :], jnp.ones((size, LANES), BF16)], axis=1)
        beta = jnp.broadcast_to(_lane_pick(gr_ref[0, rows, :], 8 * d + 4 + head), (size, LANES))
        alpha_c = gc_ref[0, c, pl.ds(4 * d + head, 1), :][:, :size]
        m_prev = m_ref[d]
        cn = cn_ref[d]
        keep = (si <= ti) if d == 0 else (si >= ti)
        dmat = jnp.where(keep, wide(beta, size) + alpha_c, -jnp.inf)
        m_inter = beta + m_prev
        m_comb = jnp.maximum(jnp.broadcast_to(jnp.max(dmat, axis=1, keepdims=True), (size, LANES)),
                             m_inter)
        s = lax.dot_general(q, k, _NT, preferred_element_type=F32)
        w = jnp.exp(dmat - wide(m_comb, size)) * s
        scale = jnp.exp(m_inter - m_comb)
        both = (jnp.dot(w.astype(BF16), v1, preferred_element_type=F32)
                + wide(scale, M_DV + LANES)
                * jnp.dot(q, cn.astype(BF16), preferred_element_type=F32)) * qk_scale
        inv = 1.0 / jnp.maximum(jnp.abs(both[:, M_DV:]), jnp.exp(-m_comb))
        hval = both[:, :M_DV] * wide(inv, M_DV)
        if d == 0:
            hf_ref[rows, :] = hval
            tot = beta[size - 1:size]
        else:
            hb_ref[rows, :] = hval
            tot = beta[0:1]
        a_max = jnp.max(alpha_c, axis=1, keepdims=True)
        m_new = jnp.maximum(tot + m_prev, tot + a_max)
        decay = jnp.exp(tot + m_prev - m_new)
        wk = jnp.exp(alpha_c + wide(tot - m_new, size))
        kw = (k.T.astype(F32) * wk).astype(BF16)
        cn_ref[d] = wide(decay, M_DV + LANES) * cn + jnp.dot(kw, v1, preferred_element_type=F32)
        m_ref[d] = m_new

    step(0, nx, TAIL)
    step(1, nx - 1, M_XCHUNK)

    def body(i, carry):
        step(0, i - 1, M_XCHUNK)
        step(1, nx - 1 - i, M_XCHUNK)
        return carry

    lax.fori_loop(1, nx, body, 0, unroll=True)
    step(0, nx - 1, M_XCHUNK)
    step(1, nx, TAIL)

    def finish(c, size):
        rows = pl.ds(_chunk_start(c), size)
        hsum = hf_ref[rows, :] + hb_ref[rows, :]
        og = o_ref[rows, :].astype(F32)
        out_ref[rows, :] = (_rms_rows(hsum, gh_ref[...]) * (1.0 / (1.0 + jnp.exp(-og)))).astype(BF16)

    def finish_body(c, carry):
        finish(c, M_XCHUNK)
        return carry

    lax.fori_loop(0, nx, finish_body, 0)
    finish(nx, TAIL)


def _mlstm(q, k, v, o, grow, gcol, g_head, seq, name):
    _, b, lp, _ = q.shape
    nblk = seq // M_XCHUNK + 1
    per_head = lambda width: pl.BlockSpec((None, None, lp, width), lambda i, h: (h, i, 0, 0))
    return pl.pallas_call(
        functools.partial(_mlstm_kernel, seq=seq),
        grid=(b, M_HEADS),
        in_specs=[
            per_head(M_DK),
            per_head(M_DK),
            per_head(M_DV),
            per_head(M_DV),
            pl.BlockSpec((1, lp, LANES), lambda i, h: (i, 0, 0)),
            pl.BlockSpec((1, nblk, 8, M_XCHUNK), lambda i, h: (i, 0, 0, 0)),
            pl.BlockSpec((1, M_DV), lambda i, h: (0, 0)),
        ],
        out_specs=per_head(M_DV),
        out_shape=jax.ShapeDtypeStruct((M_HEADS, b, lp, M_DV), BF16),
        scratch_shapes=[
            pltpu.VMEM((lp, M_DV), F32),
            pltpu.VMEM((lp, M_DV), F32),
            pltpu.VMEM((2, M_DK, M_DV + LANES), F32),
            pltpu.VMEM((2, 1, LANES), F32),
        ],
        compiler_params=_params("arbitrary", "arbitrary"),
        name=name,
    )(q, k, v, o, grow, gcol, g_head)


ATTN_TK_ONE_CHUNK = 2048
ATTN_TK = 1024
ATTN_SCORE_BYTES = 9 * 1024 * 1024


def _attn_tiles(seq, lp):
    tk = seq if seq <= ATTN_TK_ONE_CHUNK else ATTN_TK
    return _row_tile(lp, ATTN_SCORE_BYTES // (4 * tk)), tk


ATTN_MIN_LOG2_SUM = -60.0
ATTN_BOUND_SLACK = 1.02


def _attn_kernel(q_ref, k_ref, v_ref, lam_ref, gk_ref, gs_ref, out_ref, *, seq, tq, tk, lambda_init):
    lp = seq + TAIL
    q = q_ref[0]
    lane = lax.broadcasted_iota(jnp.int32, (tq, LANES), 1)
    zero = jnp.zeros_like(q)
    qs = (jnp.where(lane < D_HD, q, zero), jnp.where(lane >= D_HD, q, zero))
    kt = k_ref[0, seq:lp, :]
    vt = v_ref[0, seq:lp, :]
    tail_ok = lax.broadcasted_iota(jnp.int32, (tq, TAIL), 1) >= M_PAD
    lam_p = lam_ref[...]
    lam = (jnp.exp(jnp.sum(lam_p[0:1] * lam_p[1:2], axis=1, keepdims=True))
           - jnp.exp(jnp.sum(lam_p[2:3] * lam_p[3:4], axis=1, keepdims=True)) + lambda_init)

    def emit(acc1, l1, acc2, l2):
        o = acc1 / l1 - lam * (acc2 / l2)
        out_ref[0] = (_rms_rows(o, gs_ref[...]) * (1.0 - lambda_init)).astype(BF16)

    group = _group_ones(LANES, D_HD)
    ksq = D_HD * jnp.max(gk_ref[...] * gk_ref[...], axis=1, keepdims=True)
    qf = q.astype(F32)
    qsq = jnp.dot((qf * qf).astype(BF16), group, preferred_element_type=F32)
    bound = jnp.sqrt(qsq * ksq) * ATTN_BOUND_SLACK
    shifts = (jnp.max(jnp.where(lane < D_HD, bound, 0.0), axis=1, keepdims=True),
              jnp.max(jnp.where(lane >= D_HD, bound, 0.0), axis=1, keepdims=True))

    fast = []
    for qc, shift in zip(qs, shifts):
        s = jnp.where(tail_ok, lax.dot_general(qc, kt, _NT, preferred_element_type=F32), NEG)
        e = jnp.exp2(s - shift)
        fast += [jnp.sum(e, axis=1, keepdims=True), jnp.dot(e.astype(BF16), vt, preferred_element_type=F32)]

    def fast_body(j, st):
        rows = pl.ds(pl.multiple_of(j * tk, tk), tk)
        kc = k_ref[0, rows, :]
        vc = v_ref[0, rows, :]
        new = []
        for c, (qc, shift) in enumerate(zip(qs, shifts)):
            l, acc = st[2 * c:2 * c + 2]
            e = jnp.exp2(lax.dot_general(qc, kc, _NT, preferred_element_type=F32) - shift)
            new += [l + jnp.sum(e, axis=1, keepdims=True),
                    acc + jnp.dot(e.astype(BF16), vc, preferred_element_type=F32)]
        return tuple(new)

    l1, acc1, l2, acc2 = lax.fori_loop(0, seq // tk, fast_body, tuple(fast), unroll=True)
    emit(acc1, l1, acc2, l2)

    @pl.when(jnp.logical_not(jnp.min(jnp.minimum(l1, l2)) >= 2.0 ** ATTN_MIN_LOG2_SUM))
    def _():
        state = []
        for qc in qs:
            s = jnp.where(tail_ok, lax.dot_general(qc, kt, _NT, preferred_element_type=F32), NEG)
            m = jnp.max(s, axis=1, keepdims=True)
            e = jnp.exp2(s - m)
            state += [m, jnp.sum(e, axis=1, keepdims=True),
                      jnp.dot(e.astype(BF16), vt, preferred_element_type=F32)]

        def body(j, st):
            rows = pl.ds(pl.multiple_of(j * tk, tk), tk)
            kc = k_ref[0, rows, :]
            vc = v_ref[0, rows, :]
            new = []
            for c, qc in enumerate(qs):
                m, l, acc = st[3 * c:3 * c + 3]
                s = lax.dot_general(qc, kc, _NT, preferred_element_type=F32)
                m_new = jnp.maximum(m, jnp.max(s, axis=1, keepdims=True))
                a = jnp.exp2(m - m_new)
                e = jnp.exp2(s - m_new)
                new += [m_new, a * l + jnp.sum(e, axis=1, keepdims=True),
                        a * acc + jnp.dot(e.astype(BF16), vc, preferred_element_type=F32)]
            return tuple(new)

        _, x1, a1, _, x2, a2 = lax.fori_loop(0, seq // tk, body, tuple(state))
        emit(a1, x1, a2, x2)


def _attn(proj, lam, gk, gs, seq, lambda_init, name):
    b, lp, _ = proj.shape
    tq, tk = _attn_tiles(seq, lp)
    full = lambda i, h, j: (0, 0)
    return pl.pallas_call(
        functools.partial(_attn_kernel, seq=seq, tq=tq, tk=tk, lambda_init=lambda_init),
        grid=(b, D_HEADS, lp // tq),
        in_specs=[
            pl.BlockSpec((1, tq, LANES), lambda i, h, j: (i, j, h)),
            pl.BlockSpec((1, lp, LANES), lambda i, h, j: (i, 0, D_HEADS + h)),
            pl.BlockSpec((1, lp, LANES), lambda i, h, j: (i, 0, 2 * D_HEADS + h)),
            pl.BlockSpec((4, D_HD), full),
            pl.BlockSpec((1, D_HD), full),
            pl.BlockSpec((1, LANES), full),
        ],
        out_specs=pl.BlockSpec((1, tq, LANES), lambda i, h, j: (i, j, h)),
        out_shape=jax.ShapeDtypeStruct((b, lp, D_V_W), BF16),
        compiler_params=_params("arbitrary", "arbitrary", "arbitrary"),
        name=name,
    )(proj, proj, proj, lam, gk, gs)


def _rope_tables(seq, g_q, g_k):
    lp = seq + TAIL
    r = jnp.arange(lp)
    pos = jnp.where(r < seq, r + N_META, jnp.maximum(r - seq - M_PAD, 0)).astype(F32)
    half = ROPE_DIM // 2
    inv = ROPE_THETA ** (-jnp.arange(0, ROPE_DIM, 2, dtype=F32) / ROPE_DIM)
    ang = pos[:, None] * inv[None, :]
    cos, sin = jnp.cos(ang), jnp.sin(ang)
    rest = D_HD - ROPE_DIM

    def fold(g, scale):
        g = g.astype(F32) * scale
        cos_t = jnp.concatenate([cos * g[:half], cos * g[half:ROPE_DIM],
                                 jnp.broadcast_to(g[ROPE_DIM:], (lp, rest))], axis=1)
        sin_dn = jnp.concatenate([-sin * g[half:ROPE_DIM], jnp.zeros((lp, half + rest), F32)], axis=1)
        sin_up = jnp.concatenate([jnp.zeros((lp, half), F32), sin * g[:half],
                                  jnp.zeros((lp, rest), F32)], axis=1)
        return jnp.stack([jnp.tile(t, (1, 2)) for t in (cos_t, sin_dn, sin_up)])

    return jnp.stack([fold(g_q, D_HD ** -0.5 * math.log2(math.e)), fold(g_k, 1.0)])


def _trunk(x, meta_tokens, norm_mix, norm_ffn, m_w_main, m_w_gate, m_b_gate, m_g_head, m_w_out,
           d_w_in, d_g_q, d_g_k, d_lam, d_g_sub, d_w_out, ffn_w_in, ffn_w_out, tag):
    b, seq, d = x.shape
    lp = seq + TAIL
    rows = b * lp
    meta = jnp.broadcast_to(meta_tokens[None].astype(x.dtype), (b, N_META, d))
    h = jnp.concatenate([x, jnp.zeros((b, M_PAD, d), x.dtype), meta], axis=1)
    for i in range(DEPTH):
        j = i // 2
        gmix = norm_mix[i][None]
        hrows = h.reshape(rows, d)
        if i % 2 == 0:
            *qkvo, gates = _m_in(hrows, gmix, m_w_main[j], m_w_gate[j], m_b_gate[j], f"m_in_{tag}{i}")
            grow, gcol = _gate_prep(gates.reshape(b, lp, LANES), seq, f"m_prep_{tag}{i}")
            qkvo = [t.reshape(M_HEADS, b, lp, t.shape[-1]) for t in qkvo]
            mixed = _mlstm(*qkvo, grow, gcol, m_g_head[j][None], seq, f"m_cell_{tag}{i}")
            w_mix_out = m_w_out[j]
        else:
            lambda_init = 0.8 - 0.6 * math.exp(-0.3 * i)
            tables = _rope_tables(seq, d_g_q[j], d_g_k[j])
            proj = _qkv_rope(hrows, gmix, d_w_in[j], tables, lp, f"d_in_{tag}{i}")
            mixed = _attn(proj.reshape(b, lp, 3 * D_QK_W), d_lam[j], d_g_k[j].astype(F32)[None],
                          d_g_sub[j], seq, lambda_init, f"d_attn_{tag}{i}")
            w_mix_out = d_w_out[j]
        h = _ffn(mixed, w_mix_out, h, norm_ffn[i][None], ffn_w_in[i], ffn_w_out[i], f"ffn_{tag}{i}",
                 out_seq=seq if i == DEPTH - 1 else None)
    return h


def kernel(x_prompt, x_sample, meta_tokens, norm_mix, norm_ffn, m_w_in, m_b_gate, m_g_head, m_w_out,
           d_w_in, d_g_q, d_g_k, d_lam_q1, d_lam_k1, d_lam_q2, d_lam_k2, d_g_sub, d_w_out,
           ffn_w_in, ffn_w_out):
    n_gate = 4 * M_HEADS
    m_w_main = m_w_in[:, :, :M_MAIN_W].astype(BF16)
    m_w_gate = jnp.pad(m_w_in[:, :, M_MAIN_W:], ((0, 0), (0, 0), (0, LANES - n_gate))).astype(BF16)
    b_gate = jnp.pad(m_b_gate.astype(F32), ((0, 0), (0, LANES - n_gate)))[:, None, :]
    d_lam = jnp.stack([d_lam_q1, d_lam_k1, d_lam_q2, d_lam_k2], axis=1).astype(F32)
    args = (meta_tokens, norm_mix.astype(F32), norm_ffn.astype(F32), m_w_main, m_w_gate, b_gate,
            m_g_head.astype(F32), m_w_out.astype(BF16), d_w_in.astype(BF16), d_g_q, d_g_k,
            d_lam, d_g_sub.astype(F32)[:, None, :], d_w_out.astype(BF16), ffn_w_in.astype(BF16),
            ffn_w_out.astype(BF16))
    return (_trunk(x_prompt, *args, "p"), _trunk(x_sample, *args, "s"))
```

```python
import functools
import math

import jax
import jax.numpy as jnp
from jax import lax
from jax.experimental import pallas as pl
from jax.experimental.pallas import tpu as pltpu

F32 = jnp.float32
BF16 = jnp.bfloat16

D_MODEL = 1024
DEPTH = 4
N_META = 16
RMS_EPS = 1e-6
NEG = -1e30
M_HEADS = 4
M_DK = 128
M_DV = 256
M_CHUNK = 64
M_XCHUNK = 256
M_PAD = M_CHUNK - N_META
M_SCAN_VMEM = 32 * 1024 * 1024
M_QK_W = M_HEADS * M_DK
M_V_W = M_HEADS * M_DV
M_MAIN_W = 2 * M_QK_W + 2 * M_V_W
D_HEADS = 8
D_HD = 64
D_QK_W = D_HEADS * 2 * D_HD
D_V_W = D_HEADS * 2 * D_HD
ROPE_THETA = 500000.0
ROPE_DIM = D_HD // 4
FF = 2816
TAIL = M_CHUNK
LANES = 128
VMEM_LIMIT = 56 * 1024 * 1024


def _params(*sem):
    return pltpu.CompilerParams(dimension_semantics=sem, vmem_limit_bytes=VMEM_LIMIT)


def _row_tile(rows, target):
    best = 16
    for t in range(16, target + 1, 16):
        if rows % t == 0:
            best = t
    return best


def _rms_rows(x, g):
    ms = jnp.mean(x * x, axis=-1, keepdims=True)
    return x * lax.rsqrt(ms + RMS_EPS) * g


MM_COLS = 1024


def _m_in_kernel(x_ref, g_ref, w_ref, wg_ref, bg_ref, q_ref, k_ref, v_ref, o_ref, og_ref):
    xn = _rms_rows(x_ref[...], g_ref[...]).astype(BF16)
    outs = ((q_ref, M_DK), (k_ref, M_DK), (v_ref, M_DV), (o_ref, M_DV))
    col = 0
    for ref, width in outs:
        total = M_HEADS * width
        cols = min(MM_COLS, total)
        for c0 in range(0, total, cols):
            acc = jnp.dot(xn, w_ref[:, col + c0:col + c0 + cols],
                          preferred_element_type=F32).astype(BF16)
            for t in range(cols // width):
                ref[c0 // width + t] = acc[:, t * width:(t + 1) * width]
        col += total
    og_ref[...] = jnp.dot(xn, wg_ref[...], preferred_element_type=F32) + bg_ref[...]


def _m_in(x, g, w, wg, bg, name):
    rows, d = x.shape
    tm = _row_tile(rows, 1056)
    const = lambda i: (0, 0)
    head_major = lambda width: pl.BlockSpec((M_HEADS, tm, width), lambda i: (0, i, 0))
    widths = (M_DK, M_DK, M_DV, M_DV)
    return pl.pallas_call(
        _m_in_kernel,
        grid=(rows // tm,),
        in_specs=[
            pl.BlockSpec((tm, d), lambda i: (i, 0)),
            pl.BlockSpec((1, d), const),
            pl.BlockSpec((d, M_MAIN_W), const, pipeline_mode=pl.Buffered(1)),
            pl.BlockSpec((d, LANES), const, pipeline_mode=pl.Buffered(1)),
            pl.BlockSpec((1, LANES), const),
        ],
        out_specs=[head_major(wd) for wd in widths] + [pl.BlockSpec((tm, LANES), lambda i: (i, 0))],
        out_shape=[jax.ShapeDtypeStruct((M_HEADS, rows, wd), BF16) for wd in widths]
        + [jax.ShapeDtypeStruct((rows, LANES), F32)],
        compiler_params=_params("arbitrary"),
        name=name,
    )(x, g, w, wg, bg)


def _group_ones(width, group):
    gi = lax.broadcasted_iota(jnp.int32, (width, width), 0) // group
    gj = lax.broadcasted_iota(jnp.int32, (width, width), 1) // group
    return jnp.where(gi == gj, 1.0, 0.0).astype(BF16)


def _qkv_rope_kernel(x_ref, g_ref, w_ref, tab_ref, o_ref):
    xn = _rms_rows(x_ref[...], g_ref[...]).astype(BF16)
    wide = 2 * LANES
    group = _group_ones(wide, D_HD)
    half = ROPE_DIM // 2
    for j in range(3):
        acc = jnp.dot(xn, w_ref[:, j * D_QK_W:(j + 1) * D_QK_W], preferred_element_type=F32)
        if j == 2:
            o_ref[:, j * D_QK_W:(j + 1) * D_QK_W] = acc.astype(BF16)
            continue
        for p in range(D_QK_W // wide):
            x2 = acc[:, p * wide:(p + 1) * wide]
            ssq = jnp.dot((x2 * x2).astype(BF16), group, preferred_element_type=F32)
            xs = x2 * lax.rsqrt(ssq * (1.0 / D_HD) + RMS_EPS)
            for hf in range(2):
                xh = xs[:, hf * LANES:(hf + 1) * LANES]
                out = (xh * tab_ref[j, 0] + pltpu.roll(xh, LANES - half, 1) * tab_ref[j, 1]
                       + pltpu.roll(xh, half, 1) * tab_ref[j, 2])
                c0 = j * D_QK_W + p * wide + hf * LANES
                o_ref[:, c0:c0 + LANES] = out.astype(BF16)


def _qkv_rope(x, g, w, tables, lp, name):
    rows, d = x.shape
    n = w.shape[1]
    tm = _row_tile(lp, 1056)
    nb = lp // tm
    return pl.pallas_call(
        _qkv_rope_kernel,
        grid=(rows // tm,),
        in_specs=[
            pl.BlockSpec((tm, d), lambda i: (i, 0)),
            pl.BlockSpec((1, d), lambda i: (0, 0)),
            pl.BlockSpec((d, n), lambda i: (0, 0), pipeline_mode=pl.Buffered(1)),
            pl.BlockSpec((2, 3, tm, LANES), lambda i: (0, 0, i % nb, 0)),
        ],
        out_specs=pl.BlockSpec((tm, n), lambda i: (i, 0)),
        out_shape=jax.ShapeDtypeStruct((rows, n), BF16),
        compiler_params=_params("arbitrary"),
        name=name,
    )(x, g, w, tables)


FF_CHUNK = 512
FFN_ROWS = 512


def _ffn_kernel(a_ref, wm_ref, h_ref, g_ref, wi_ref, wo_ref, o_ref, xn_ref, acc_ref):
    if len(a_ref.shape) == 3:
        a = jnp.concatenate([a_ref[hd] for hd in range(a_ref.shape[0])], axis=1)
    else:
        a = a_ref[...]
    x = h_ref[...] + jnp.dot(a, wm_ref[...], preferred_element_type=F32)
    xn_ref[...] = _rms_rows(x, g_ref[...]).astype(BF16)
    acc_ref[...] = x
    for c0 in range(0, FF, FF_CHUNK):
        c1 = min(c0 + FF_CHUNK, FF)
        xn = xn_ref[...]
        gate = jnp.dot(xn, wi_ref[:, c0:c1], preferred_element_type=F32)
        up = jnp.dot(xn, wi_ref[:, FF + c0:FF + c1], preferred_element_type=F32)
        act = (gate * (1.0 / (1.0 + jnp.exp(-gate))) * up).astype(BF16)
        acc_ref[...] += jnp.dot(act, wo_ref[c0:c1, :], preferred_element_type=F32)
    o_ref[...] = acc_ref[...]


def _ffn(a, wm, h, g, wi, wo, name, out_seq=None):
    b, lp, d = h.shape
    const = lambda *_: (0, 0)
    weights = [
        pl.BlockSpec((d, d), const, pipeline_mode=pl.Buffered(1)),
        pl.BlockSpec((1, d), const),
        pl.BlockSpec((d, 2 * FF), const, pipeline_mode=pl.Buffered(1)),
        pl.BlockSpec((FF, d), const, pipeline_mode=pl.Buffered(1)),
    ]
    if out_seq is None:
        rows = b * lp
        tm = _row_tile(rows, FFN_ROWS)
        grid = (rows // tm,)
        tile = pl.BlockSpec((tm, d), lambda i: (i, 0))
        h = h.reshape(rows, d)
        if a.ndim == 4:
            a = a.reshape(a.shape[0], rows, a.shape[3])
            a_tile = pl.BlockSpec((a.shape[0], tm, a.shape[2]), lambda i: (0, i, 0))
        else:
            a, a_tile = a.reshape(rows, d), tile
        out_shape = jax.ShapeDtypeStruct((rows, d), F32)
        sem = ("arbitrary",)
    else:
        tm = _row_tile(out_seq, FFN_ROWS)
        grid = (b, out_seq // tm)
        tile = a_tile = pl.BlockSpec((None, tm, d), lambda i, j: (i, j, 0))
        if a.ndim == 4:
            a_tile = pl.BlockSpec((a.shape[0], None, tm, a.shape[3]), lambda i, j: (0, i, j, 0))
        out_shape = jax.ShapeDtypeStruct((b, out_seq, d), F32)
        sem = ("arbitrary", "arbitrary")
    out = pl.pallas_call(
        _ffn_kernel,
        grid=grid,
        in_specs=[a_tile, weights[0], tile, weights[1], weights[2], weights[3]],
        out_specs=tile,
        out_shape=out_shape,
        scratch_shapes=[pltpu.VMEM((tm, d), BF16), pltpu.VMEM((tm, d), F32)],
        compiler_params=_params(*sem),
        name=name,
    )(a, wm, h, g, wi, wo)
    return out if out_seq is not None else out.reshape(b, lp, d)


def _split3(x):
    hi = x.astype(BF16)
    r1 = x - hi.astype(F32)
    mid = r1.astype(BF16)
    lo = (r1 - mid.astype(F32)).astype(BF16)
    return hi, mid, lo


def _dot_exact01(m01, x, dims):
    out = None
    for part in _split3(x):
        t = lax.dot_general(m01, part, dims, preferred_element_type=F32)
        out = t if out is None else out + t
    return out


_NN = (((1,), (0,)), ((), ()))
_NT = (((1,), (1,)), ((), ()))


def _gate_prep_kernel(g_ref, row_ref, col_ref, *, seq):
    nx = seq // M_XCHUNK
    er = lax.broadcasted_iota(jnp.int32, (8, LANES), 0)
    el = lax.broadcasted_iota(jnp.int32, (8, LANES), 1)
    pick = jnp.where(el == jnp.where(er < 4, er, er + 4), 1.0, 0.0).astype(BF16)
    for j in range(nx + 1):
        size = M_XCHUNK if j < nx else TAIL
        rows = slice(j * M_XCHUNK, j * M_XCHUNK + size)
        g = g_ref[0, rows, :]
        logf = jnp.minimum(g, 0.0) - jnp.log(1.0 + jnp.exp(-jnp.abs(g)))
        ig = g
        if j == nx:
            real = lax.broadcasted_iota(jnp.int32, (size, LANES), 0) >= M_PAD
            logf = jnp.where(real, logf, 0.0)
            ig = jnp.where(real, g, NEG)
        ti = lax.broadcasted_iota(jnp.int32, (size, size), 0)
        si = lax.broadcasted_iota(jnp.int32, (size, size), 1)
        pre_m = jnp.where(si <= ti, 1.0, 0.0).astype(BF16)
        pre = _dot_exact01(pre_m, logf, _NN)
        suf = pre[size - 1:size, :] - pre + logf
        lane = lax.broadcasted_iota(jnp.int32, (size, LANES), 1)
        pre_dn = pltpu.roll(pre, LANES - M_HEADS, 1)
        suf_dn = pltpu.roll(suf, LANES - M_HEADS, 1)
        row = jnp.where(lane < 4, ig - pre_dn,
                        jnp.where(lane < 8, pre,
                                  jnp.where(lane < 12, ig - suf_dn, suf)))
        row = jnp.where(lane < 16, row, 0.0)
        row_ref[0, rows, :] = row
        col = _dot_exact01(pick, row, _NT)
        if j == nx:
            col_ref[0, j] = jnp.zeros((8, M_XCHUNK), F32)
            col_ref[0, j, :, 0:size] = col
        else:
            col_ref[0, j] = col


def _gate_prep(g, seq, name):
    b, lp, _ = g.shape
    nblk = seq // M_XCHUNK + 1
    return pl.pallas_call(
        functools.partial(_gate_prep_kernel, seq=seq),
        grid=(b,),
        in_specs=[pl.BlockSpec((1, lp, LANES), lambda i: (i, 0, 0))],
        out_specs=[
            pl.BlockSpec((1, lp, LANES), lambda i: (i, 0, 0)),
            pl.BlockSpec((1, nblk, 8, M_XCHUNK), lambda i: (i, 0, 0, 0)),
        ],
        out_shape=[
            jax.ShapeDtypeStruct((b, lp, LANES), F32),
            jax.ShapeDtypeStruct((b, nblk, 8, M_XCHUNK), F32),
        ],
        compiler_params=_params("arbitrary"),
        name=name,
    )(g)


def _lane_pick(x, idx):
    lane = lax.broadcasted_iota(jnp.int32, x.shape, 1)
    return jnp.sum(jnp.where(lane == idx, x, 0.0), axis=1, keepdims=True)


def _chunk_start(c):
    return c * M_XCHUNK if isinstance(c, int) else pl.multiple_of(c * M_XCHUNK, M_XCHUNK)


def _mlstm_kernel(q_ref, k_ref, v_ref, o_ref, gr_ref, gc_ref, gh_ref, out_ref,
                  hf_ref, hb_ref, cn_ref, m_ref, *, seq, heads_per_step):
    nx = seq // M_XCHUNK
    qk_scale = M_DK ** -0.5
    cn_ref[...] = jnp.zeros_like(cn_ref)
    m_ref[...] = jnp.zeros_like(m_ref)

    def step(d, c, size):
        for hh in range(heads_per_step):
            step_head(d, c, size, hh, pl.program_id(1) * heads_per_step + hh)

    def step_head(d, c, size, hh, head):
        def wide(x, n):
            return x[:, :n] if n < LANES else jnp.concatenate([x] * (n // LANES), axis=1)

        rows = pl.ds(_chunk_start(c), size)
        ti = lax.broadcasted_iota(jnp.int32, (size, size), 0)
        si = lax.broadcasted_iota(jnp.int32, (size, size), 1)
        q = q_ref[hh, rows, :]
        k = k_ref[hh, rows, :]
        v1 = jnp.concatenate([v_ref[hh, rows, :], jnp.ones((size, LANES), BF16)], axis=1)
        beta = jnp.broadcast_to(_lane_pick(gr_ref[0, rows, :], 8 * d + 4 + head), (size, LANES))
        alpha_c = gc_ref[0, c, pl.ds(4 * d + head, 1), :][:, :size]
        m_prev = m_ref[hh, d]
        cn = cn_ref[hh, d]
        keep = (si <= ti) if d == 0 else (si >= ti)
        dmat = jnp.where(keep, wide(beta, size) + alpha_c, -jnp.inf)
        m_inter = beta + m_prev
        m_comb = jnp.maximum(jnp.broadcast_to(jnp.max(dmat, axis=1, keepdims=True), (size, LANES)),
                             m_inter)
        s = lax.dot_general(q, k, _NT, preferred_element_type=F32)
        w = jnp.exp(dmat - wide(m_comb, size)) * s
        scale = jnp.exp(m_inter - m_comb)
        both = (jnp.dot(w.astype(BF16), v1, preferred_element_type=F32)
                + wide(scale, M_DV + LANES)
                * jnp.dot(q, cn.astype(BF16), preferred_element_type=F32)) * qk_scale
        inv = 1.0 / jnp.maximum(jnp.abs(both[:, M_DV:]), jnp.exp(-m_comb))
        hval = both[:, :M_DV] * wide(inv, M_DV)
        if d == 0:
            hf_ref[hh, rows, :] = hval
            tot = beta[size - 1:size]
        else:
            hb_ref[hh, rows, :] = hval
            tot = beta[0:1]
        a_max = jnp.max(alpha_c, axis=1, keepdims=True)
        m_new = jnp.maximum(tot + m_prev, tot + a_max)
        decay = jnp.exp(tot + m_prev - m_new)
        wk = jnp.exp(alpha_c + wide(tot - m_new, size))
        kw = (k.T.astype(F32) * wk).astype(BF16)
        cn_ref[hh, d] = wide(decay, M_DV + LANES) * cn + jnp.dot(kw, v1, preferred_element_type=F32)
        m_ref[hh, d] = m_new

    step(0, nx, TAIL)
    step(1, nx - 1, M_XCHUNK)

    def body(i, carry):
        step(0, i - 1, M_XCHUNK)
        step(1, nx - 1 - i, M_XCHUNK)
        return carry

    lax.fori_loop(1, nx, body, 0, unroll=True)
    step(0, nx - 1, M_XCHUNK)
    step(1, nx, TAIL)

    def finish(c, size):
        rows = pl.ds(_chunk_start(c), size)
        for hh in range(heads_per_step):
            hsum = hf_ref[hh, rows, :] + hb_ref[hh, rows, :]
            og = o_ref[hh, rows, :].astype(F32)
            out_ref[hh, rows, :] = (_rms_rows(hsum, gh_ref[...])
                                    * (1.0 / (1.0 + jnp.exp(-og)))).astype(BF16)

    def finish_body(c, carry):
        finish(c, M_XCHUNK)
        return carry

    lax.fori_loop(0, nx, finish_body, 0)
    finish(nx, TAIL)


def _mlstm(q, k, v, o, grow, gcol, g_head, seq, name):
    _, b, lp, _ = q.shape
    nblk = seq // M_XCHUNK + 1
    head_bytes = lp * (2 * 2 * (2 * M_DK + 3 * M_DV) + 2 * 4 * M_DV)
    g = 2 if 2 * head_bytes <= M_SCAN_VMEM else 1
    per_head = lambda width: pl.BlockSpec((g, None, lp, width), lambda i, h: (h, i, 0, 0))
    return pl.pallas_call(
        functools.partial(_mlstm_kernel, seq=seq, heads_per_step=g),
        grid=(b, M_HEADS // g),
        in_specs=[
            per_head(M_DK),
            per_head(M_DK),
            per_head(M_DV),
            per_head(M_DV),
            pl.BlockSpec((1, lp, LANES), lambda i, h: (i, 0, 0)),
            pl.BlockSpec((1, nblk, 8, M_XCHUNK), lambda i, h: (i, 0, 0, 0)),
            pl.BlockSpec((1, M_DV), lambda i, h: (0, 0)),
        ],
        out_specs=per_head(M_DV),
        out_shape=jax.ShapeDtypeStruct((M_HEADS, b, lp, M_DV), BF16),
        scratch_shapes=[
            pltpu.VMEM((g, lp, M_DV), F32),
            pltpu.VMEM((g, lp, M_DV), F32),
            pltpu.VMEM((g, 2, M_DK, M_DV + LANES), F32),
            pltpu.VMEM((g, 2, 1, LANES), F32),
        ],
        compiler_params=_params("arbitrary", "arbitrary"),
        name=name,
    )(q, k, v, o, grow, gcol, g_head)


ATTN_TK_ONE_CHUNK = 2048
ATTN_TK = 1024
ATTN_SCORE_BYTES = 9 * 1024 * 1024


def _attn_tiles(seq, lp):
    tk = seq if seq <= ATTN_TK_ONE_CHUNK else ATTN_TK
    return _row_tile(lp, ATTN_SCORE_BYTES // (4 * tk)), tk


ATTN_MIN_LOG2_SUM = -60.0
ATTN_BOUND_SLACK = 1.02


def _attn_kernel(q_ref, k_ref, v_ref, lam_ref, gk_ref, gs_ref, out_ref, *, seq, tq, tk, lambda_init):
    lp = seq + TAIL
    q = q_ref[0]
    lane = lax.broadcasted_iota(jnp.int32, (tq, LANES), 1)
    zero = jnp.zeros_like(q)
    qs = (jnp.where(lane < D_HD, q, zero), jnp.where(lane >= D_HD, q, zero))
    kt = k_ref[0, seq:lp, :]
    vt = v_ref[0, seq:lp, :]
    tail_ok = lax.broadcasted_iota(jnp.int32, (tq, TAIL), 1) >= M_PAD
    lam_p = lam_ref[...]
    lam = (jnp.exp(jnp.sum(lam_p[0:1] * lam_p[1:2], axis=1, keepdims=True))
           - jnp.exp(jnp.sum(lam_p[2:3] * lam_p[3:4], axis=1, keepdims=True)) + lambda_init)

    def emit(acc1, l1, acc2, l2):
        o = acc1 / l1 - lam * (acc2 / l2)
        out_ref[0] = (_rms_rows(o, gs_ref[...]) * (1.0 - lambda_init)).astype(BF16)

    group = _group_ones(LANES, D_HD)
    ksq = D_HD * jnp.max(gk_ref[...] * gk_ref[...], axis=1, keepdims=True)
    qf = q.astype(F32)
    qsq = jnp.dot((qf * qf).astype(BF16), group, preferred_element_type=F32)
    bound = jnp.sqrt(qsq * ksq) * ATTN_BOUND_SLACK
    shifts = (jnp.max(jnp.where(lane < D_HD, bound, 0.0), axis=1, keepdims=True),
              jnp.max(jnp.where(lane >= D_HD, bound, 0.0), axis=1, keepdims=True))

    fast = []
    for qc, shift in zip(qs, shifts):
        s = jnp.where(tail_ok, lax.dot_general(qc, kt, _NT, preferred_element_type=F32), NEG)
        e = jnp.exp2(s - shift)
        fast += [jnp.sum(e, axis=1, keepdims=True), jnp.dot(e.astype(BF16), vt, preferred_element_type=F32)]

    def fast_body(j, st):
        rows = pl.ds(pl.multiple_of(j * tk, tk), tk)
        kc = k_ref[0, rows, :]
        vc = v_ref[0, rows, :]
        new = []
        for c, (qc, shift) in enumerate(zip(qs, shifts)):
            l, acc = st[2 * c:2 * c + 2]
            e = jnp.exp2(lax.dot_general(qc, kc, _NT, preferred_element_type=F32) - shift)
            new += [l + jnp.sum(e, axis=1, keepdims=True),
                    acc + jnp.dot(e.astype(BF16), vc, preferred_element_type=F32)]
        return tuple(new)

    l1, acc1, l2, acc2 = lax.fori_loop(0, seq // tk, fast_body, tuple(fast), unroll=True)
    emit(acc1, l1, acc2, l2)

    @pl.when(jnp.logical_not(jnp.min(jnp.minimum(l1, l2)) >= 2.0 ** ATTN_MIN_LOG2_SUM))
    def _():
        state = []
        for qc in qs:
            s = jnp.where(tail_ok, lax.dot_general(qc, kt, _NT, preferred_element_type=F32), NEG)
            m = jnp.max(s, axis=1, keepdims=True)
            e = jnp.exp2(s - m)
            state += [m, jnp.sum(e, axis=1, keepdims=True),
                      jnp.dot(e.astype(BF16), vt, preferred_element_type=F32)]

        def body(j, st):
            rows = pl.ds(pl.multiple_of(j * tk, tk), tk)
            kc = k_ref[0, rows, :]
            vc = v_ref[0, rows, :]
            new = []
            for c, qc in enumerate(qs):
                m, l, acc = st[3 * c:3 * c + 3]
                s = lax.dot_general(qc, kc, _NT, preferred_element_type=F32)
                m_new = jnp.maximum(m, jnp.max(s, axis=1, keepdims=True))
                a = jnp.exp2(m - m_new)
                e = jnp.exp2(s - m_new)
                new += [m_new, a * l + jnp.sum(e, axis=1, keepdims=True),
                        a * acc + jnp.dot(e.astype(BF16), vc, preferred_element_type=F32)]
            return tuple(new)

        _, x1, a1, _, x2, a2 = lax.fori_loop(0, seq // tk, body, tuple(state))
        emit(a1, x1, a2, x2)


def _attn(proj, lam, gk, gs, seq, lambda_init, name):
    b, lp, _ = proj.shape
    tq, tk = _attn_tiles(seq, lp)
    full = lambda i, h, j: (0, 0)
    return pl.pallas_call(
        functools.partial(_attn_kernel, seq=seq, tq=tq, tk=tk, lambda_init=lambda_init),
        grid=(b, D_HEADS, lp // tq),
        in_specs=[
            pl.BlockSpec((1, tq, LANES), lambda i, h, j: (i, j, h)),
            pl.BlockSpec((1, lp, LANES), lambda i, h, j: (i, 0, D_HEADS + h)),
            pl.BlockSpec((1, lp, LANES), lambda i, h, j: (i, 0, 2 * D_HEADS + h)),
            pl.BlockSpec((4, D_HD), full),
            pl.BlockSpec((1, D_HD), full),
            pl.BlockSpec((1, LANES), full),
        ],
        out_specs=pl.BlockSpec((1, tq, LANES), lambda i, h, j: (i, j, h)),
        out_shape=jax.ShapeDtypeStruct((b, lp, D_V_W), BF16),
        compiler_params=_params("arbitrary", "arbitrary", "arbitrary"),
        name=name,
    )(proj, proj, proj, lam, gk, gs)


def _rope_tables(seq, g_q, g_k):
    lp = seq + TAIL
    r = jnp.arange(lp)
    pos = jnp.where(r < seq, r + N_META, jnp.maximum(r - seq - M_PAD, 0)).astype(F32)
    half = ROPE_DIM // 2
    inv = ROPE_THETA ** (-jnp.arange(0, ROPE_DIM, 2, dtype=F32) / ROPE_DIM)
    ang = pos[:, None] * inv[None, :]
    cos, sin = jnp.cos(ang), jnp.sin(ang)
    rest = D_HD - ROPE_DIM

    def fold(g, scale):
        g = g.astype(F32) * scale
        cos_t = jnp.concatenate([cos * g[:half], cos * g[half:ROPE_DIM],
                                 jnp.broadcast_to(g[ROPE_DIM:], (lp, rest))], axis=1)
        sin_dn = jnp.concatenate([-sin * g[half:ROPE_DIM], jnp.zeros((lp, half + rest), F32)], axis=1)
        sin_up = jnp.concatenate([jnp.zeros((lp, half), F32), sin * g[:half],
                                  jnp.zeros((lp, rest), F32)], axis=1)
        return jnp.stack([jnp.tile(t, (1, 2)) for t in (cos_t, sin_dn, sin_up)])

    return jnp.stack([fold(g_q, D_HD ** -0.5 * math.log2(math.e)), fold(g_k, 1.0)])


def _trunk(x, meta_tokens, norm_mix, norm_ffn, m_w_main, m_w_gate, m_b_gate, m_g_head, m_w_out,
           d_w_in, d_g_q, d_g_k, d_lam, d_g_sub, d_w_out, ffn_w_in, ffn_w_out, tag):
    b, seq, d = x.shape
    lp = seq + TAIL
    rows = b * lp
    meta = jnp.broadcast_to(meta_tokens[None].astype(x.dtype), (b, N_META, d))
    h = jnp.concatenate([x, jnp.zeros((b, M_PAD, d), x.dtype), meta], axis=1)
    for i in range(DEPTH):
        j = i // 2
        gmix = norm_mix[i][None]
        hrows = h.reshape(rows, d)
        if i % 2 == 0:
            *qkvo, gates = _m_in(hrows, gmix, m_w_main[j], m_w_gate[j], m_b_gate[j], f"m_in_{tag}{i}")
            grow, gcol = _gate_prep(gates.reshape(b, lp, LANES), seq, f"m_prep_{tag}{i}")
            qkvo = [t.reshape(M_HEADS, b, lp, t.shape[-1]) for t in qkvo]
            mixed = _mlstm(*qkvo, grow, gcol, m_g_head[j][None], seq, f"m_cell_{tag}{i}")
            w_mix_out = m_w_out[j]
        else:
            lambda_init = 0.8 - 0.6 * math.exp(-0.3 * i)
            tables = _rope_tables(seq, d_g_q[j], d_g_k[j])
            proj = _qkv_rope(hrows, gmix, d_w_in[j], tables, lp, f"d_in_{tag}{i}")
            mixed = _attn(proj.reshape(b, lp, 3 * D_QK_W), d_lam[j], d_g_k[j].astype(F32)[None],
                          d_g_sub[j], seq, lambda_init, f"d_attn_{tag}{i}")
            w_mix_out = d_w_out[j]
        h = _ffn(mixed, w_mix_out, h, norm_ffn[i][None], ffn_w_in[i], ffn_w_out[i], f"ffn_{tag}{i}",
                 out_seq=seq if i == DEPTH - 1 else None)
    return h


def kernel(x_prompt, x_sample, meta_tokens, norm_mix, norm_ffn, m_w_in, m_b_gate, m_g_head, m_w_out,
           d_w_in, d_g_q, d_g_k, d_lam_q1, d_lam_k1, d_lam_q2, d_lam_k2, d_g_sub, d_w_out,
           ffn_w_in, ffn_w_out):
    n_gate = 4 * M_HEADS
    m_w_main = m_w_in[:, :, :M_MAIN_W].astype(BF16)
    m_w_gate = jnp.pad(m_w_in[:, :, M_MAIN_W:], ((0, 0), (0, 0), (0, LANES - n_gate))).astype(BF16)
    b_gate = jnp.pad(m_b_gate.astype(F32), ((0, 0), (0, LANES - n_gate)))[:, None, :]
    d_lam = jnp.stack([d_lam_q1, d_lam_k1, d_lam_q2, d_lam_k2], axis=1).astype(F32)
    args = (meta_tokens, norm_mix.astype(F32), norm_ffn.astype(F32), m_w_main, m_w_gate, b_gate,
            m_g_head.astype(F32), m_w_out.astype(BF16), d_w_in.astype(BF16), d_g_q, d_g_k,
            d_lam, d_g_sub.astype(F32)[:, None, :], d_w_out.astype(BF16), ffn_w_in.astype(BF16),
            ffn_w_out.astype(BF16))
    return (_trunk(x_prompt, *args, "p"), _trunk(x_sample, *args, "s"))
```

```python
import functools
import math

import jax
import jax.numpy as jnp
from jax import lax
from jax.experimental import pallas as pl
from jax.experimental.pallas import tpu as pltpu

F32 = jnp.float32
BF16 = jnp.bfloat16

D_MODEL = 1024
DEPTH = 4
N_META = 16
RMS_EPS = 1e-6
NEG = -1e30
M_HEADS = 4
M_DK = 128
M_DV = 256
M_CHUNK = 64
M_XCHUNK = 256
M_PAD = M_CHUNK - N_META
M_QK_W = M_HEADS * M_DK
M_V_W = M_HEADS * M_DV
M_MAIN_W = 2 * M_QK_W + 2 * M_V_W
D_HEADS = 8
D_HD = 64
D_QK_W = D_HEADS * 2 * D_HD
D_V_W = D_HEADS * 2 * D_HD
ROPE_THETA = 500000.0
ROPE_DIM = D_HD // 4
FF = 2816
TAIL = M_CHUNK
LANES = 128
VMEM_LIMIT = 56 * 1024 * 1024


def _params(*sem):
    return pltpu.CompilerParams(dimension_semantics=sem, vmem_limit_bytes=VMEM_LIMIT)


def _row_tile(rows, target):
    best = 16
    for t in range(16, target + 1, 16):
        if rows % t == 0:
            best = t
    return best


def _rms_rows(x, g):
    ms = jnp.mean(x * x, axis=-1, keepdims=True)
    return x * lax.rsqrt(ms + RMS_EPS) * g


MM_COLS = 1024


def _m_in_kernel(x_ref, g_ref, w_ref, wg_ref, bg_ref, q_ref, k_ref, v_ref, o_ref, og_ref):
    xn = _rms_rows(x_ref[...], g_ref[...]).astype(BF16)
    outs = ((q_ref, M_DK), (k_ref, M_DK), (v_ref, M_DV), (o_ref, M_DV))
    col = 0
    for ref, width in outs:
        total = M_HEADS * width
        cols = min(MM_COLS, total)
        for c0 in range(0, total, cols):
            acc = jnp.dot(xn, w_ref[:, col + c0:col + c0 + cols],
                          preferred_element_type=F32).astype(BF16)
            for t in range(cols // width):
                ref[c0 // width + t] = acc[:, t * width:(t + 1) * width]
        col += total
    og_ref[...] = jnp.dot(xn, wg_ref[...], preferred_element_type=F32) + bg_ref[...]


def _m_in(x, g, w, wg, bg, name):
    rows, d = x.shape
    tm = _row_tile(rows, 1056)
    const = lambda i: (0, 0)
    head_major = lambda width: pl.BlockSpec((M_HEADS, tm, width), lambda i: (0, i, 0))
    widths = (M_DK, M_DK, M_DV, M_DV)
    return pl.pallas_call(
        _m_in_kernel,
        grid=(rows // tm,),
        in_specs=[
            pl.BlockSpec((tm, d), lambda i: (i, 0)),
            pl.BlockSpec((1, d), const),
            pl.BlockSpec((d, M_MAIN_W), const, pipeline_mode=pl.Buffered(1)),
            pl.BlockSpec((d, LANES), const, pipeline_mode=pl.Buffered(1)),
            pl.BlockSpec((1, LANES), const),
        ],
        out_specs=[head_major(wd) for wd in widths] + [pl.BlockSpec((tm, LANES), lambda i: (i, 0))],
        out_shape=[jax.ShapeDtypeStruct((M_HEADS, rows, wd), BF16) for wd in widths]
        + [jax.ShapeDtypeStruct((rows, LANES), F32)],
        compiler_params=_params("arbitrary"),
        name=name,
    )(x, g, w, wg, bg)


def _group_ones(width, group):
    gi = lax.broadcasted_iota(jnp.int32, (width, width), 0) // group
    gj = lax.broadcasted_iota(jnp.int32, (width, width), 1) // group
    return jnp.where(gi == gj, 1.0, 0.0).astype(BF16)


def _qkv_rope_kernel(x_ref, g_ref, w_ref, tab_ref, o_ref):
    xn = _rms_rows(x_ref[...], g_ref[...]).astype(BF16)
    wide = 2 * LANES
    group = _group_ones(wide, D_HD)
    half = ROPE_DIM // 2
    for j in range(3):
        acc = jnp.dot(xn, w_ref[:, j * D_QK_W:(j + 1) * D_QK_W], preferred_element_type=F32)
        if j == 2:
            o_ref[:, j * D_QK_W:(j + 1) * D_QK_W] = acc.astype(BF16)
            continue
        for p in range(D_QK_W // wide):
            x2 = acc[:, p * wide:(p + 1) * wide]
            ssq = jnp.dot((x2 * x2).astype(BF16), group, preferred_element_type=F32)
            xs = x2 * lax.rsqrt(ssq * (1.0 / D_HD) + RMS_EPS)
            for hf in range(2):
                xh = xs[:, hf * LANES:(hf + 1) * LANES]
                out = (xh * tab_ref[j, 0] + pltpu.roll(xh, LANES - half, 1) * tab_ref[j, 1]
                       + pltpu.roll(xh, half, 1) * tab_ref[j, 2])
                c0 = j * D_QK_W + p * wide + hf * LANES
                o_ref[:, c0:c0 + LANES] = out.astype(BF16)


def _qkv_rope(x, g, w, tables, lp, name):
    rows, d = x.shape
    n = w.shape[1]
    tm = _row_tile(lp, 1056)
    nb = lp // tm
    return pl.pallas_call(
        _qkv_rope_kernel,
        grid=(rows // tm,),
        in_specs=[
            pl.BlockSpec((tm, d), lambda i: (i, 0)),
            pl.BlockSpec((1, d), lambda i: (0, 0)),
            pl.BlockSpec((d, n), lambda i: (0, 0), pipeline_mode=pl.Buffered(1)),
            pl.BlockSpec((2, 3, tm, LANES), lambda i: (0, 0, i % nb, 0)),
        ],
        out_specs=pl.BlockSpec((tm, n), lambda i: (i, 0)),
        out_shape=jax.ShapeDtypeStruct((rows, n), BF16),
        compiler_params=_params("arbitrary"),
        name=name,
    )(x, g, w, tables)


FF_CHUNK = 512
FFN_ROWS = 512


def _ffn_kernel(a_ref, wm_ref, h_ref, g_ref, wi_ref, wo_ref, o_ref, xn_ref):
    if len(a_ref.shape) == 3:
        a = jnp.concatenate([a_ref[hd] for hd in range(a_ref.shape[0])], axis=1)
    else:
        a = a_ref[...]
    x = h_ref[...] + jnp.dot(a, wm_ref[...], preferred_element_type=F32)
    xn_ref[...] = _rms_rows(x, g_ref[...]).astype(BF16)
    o_ref[...] = x
    for c0 in range(0, FF, FF_CHUNK):
        c1 = min(c0 + FF_CHUNK, FF)
        xn = xn_ref[...]
        gate = jnp.dot(xn, wi_ref[:, c0:c1], preferred_element_type=F32)
        up = jnp.dot(xn, wi_ref[:, FF + c0:FF + c1], preferred_element_type=F32)
        act = (gate * (1.0 / (1.0 + jnp.exp(-gate))) * up).astype(BF16)
        o_ref[...] += jnp.dot(act, wo_ref[c0:c1, :], preferred_element_type=F32)


def _ffn(a, wm, h, g, wi, wo, name, out_seq=None):
    b, lp, d = h.shape
    const = lambda *_: (0, 0)
    weights = [
        pl.BlockSpec((d, d), const, pipeline_mode=pl.Buffered(1)),
        pl.BlockSpec((1, d), const),
        pl.BlockSpec((d, 2 * FF), const, pipeline_mode=pl.Buffered(1)),
        pl.BlockSpec((FF, d), const, pipeline_mode=pl.Buffered(1)),
    ]
    if out_seq is None:
        rows = b * lp
        tm = _row_tile(rows, FFN_ROWS)
        grid = (rows // tm,)
        tile = pl.BlockSpec((tm, d), lambda i: (i, 0))
        h = h.reshape(rows, d)
        if a.ndim == 4:
            a = a.reshape(a.shape[0], rows, a.shape[3])
            a_tile = pl.BlockSpec((a.shape[0], tm, a.shape[2]), lambda i: (0, i, 0))
        else:
            a, a_tile = a.reshape(rows, d), tile
        out_shape = jax.ShapeDtypeStruct((rows, d), F32)
        sem = ("arbitrary",)
    else:
        tm = _row_tile(out_seq, FFN_ROWS)
        grid = (b, out_seq // tm)
        tile = a_tile = pl.BlockSpec((None, tm, d), lambda i, j: (i, j, 0))
        if a.ndim == 4:
            a_tile = pl.BlockSpec((a.shape[0], None, tm, a.shape[3]), lambda i, j: (0, i, j, 0))
        out_shape = jax.ShapeDtypeStruct((b, out_seq, d), F32)
        sem = ("arbitrary", "arbitrary")
    out = pl.pallas_call(
        _ffn_kernel,
        grid=grid,
        in_specs=[a_tile, weights[0], tile, weights[1], weights[2], weights[3]],
        out_specs=tile,
        out_shape=out_shape,
        scratch_shapes=[pltpu.VMEM((tm, d), BF16)],
        compiler_params=_params(*sem),
        name=name,
    )(a, wm, h, g, wi, wo)
    return out if out_seq is not None else out.reshape(b, lp, d)


def _split3(x):
    hi = x.astype(BF16)
    r1 = x - hi.astype(F32)
    mid = r1.astype(BF16)
    lo = (r1 - mid.astype(F32)).astype(BF16)
    return hi, mid, lo


def _dot_exact01(m01, x, dims):
    out = None
    for part in _split3(x):
        t = lax.dot_general(m01, part, dims, preferred_element_type=F32)
        out = t if out is None else out + t
    return out


_NN = (((1,), (0,)), ((), ()))
_NT = (((1,), (1,)), ((), ()))


def _gate_prep_kernel(g_ref, row_ref, col_ref, *, seq):
    nx = seq // M_XCHUNK
    er = lax.broadcasted_iota(jnp.int32, (8, LANES), 0)
    el = lax.broadcasted_iota(jnp.int32, (8, LANES), 1)
    pick = jnp.where(el == jnp.where(er < 4, er, er + 4), 1.0, 0.0).astype(BF16)
    for j in range(nx + 1):
        size = M_XCHUNK if j < nx else TAIL
        rows = slice(j * M_XCHUNK, j * M_XCHUNK + size)
        g = g_ref[0, rows, :]
        logf = jnp.minimum(g, 0.0) - jnp.log(1.0 + jnp.exp(-jnp.abs(g)))
        ig = g
        if j == nx:
            real = lax.broadcasted_iota(jnp.int32, (size, LANES), 0) >= M_PAD
            logf = jnp.where(real, logf, 0.0)
            ig = jnp.where(real, g, NEG)
        ti = lax.broadcasted_iota(jnp.int32, (size, size), 0)
        si = lax.broadcasted_iota(jnp.int32, (size, size), 1)
        pre_m = jnp.where(si <= ti, 1.0, 0.0).astype(BF16)
        pre = _dot_exact01(pre_m, logf, _NN)
        suf = pre[size - 1:size, :] - pre + logf
        lane = lax.broadcasted_iota(jnp.int32, (size, LANES), 1)
        pre_dn = pltpu.roll(pre, LANES - M_HEADS, 1)
        suf_dn = pltpu.roll(suf, LANES - M_HEADS, 1)
        row = jnp.where(lane < 4, ig - pre_dn,
                        jnp.where(lane < 8, pre,
                                  jnp.where(lane < 12, ig - suf_dn, suf)))
        row = jnp.where(lane < 16, row, 0.0)
        row_ref[0, rows, :] = row
        col = _dot_exact01(pick, row, _NT)
        if j == nx:
            col_ref[0, j] = jnp.zeros((8, M_XCHUNK), F32)
            col_ref[0, j, :, 0:size] = col
        else:
            col_ref[0, j] = col


def _gate_prep(g, seq, name):
    b, lp, _ = g.shape
    nblk = seq // M_XCHUNK + 1
    return pl.pallas_call(
        functools.partial(_gate_prep_kernel, seq=seq),
        grid=(b,),
        in_specs=[pl.BlockSpec((1, lp, LANES), lambda i: (i, 0, 0))],
        out_specs=[
            pl.BlockSpec((1, lp, LANES), lambda i: (i, 0, 0)),
            pl.BlockSpec((1, nblk, 8, M_XCHUNK), lambda i: (i, 0, 0, 0)),
        ],
        out_shape=[
            jax.ShapeDtypeStruct((b, lp, LANES), F32),
            jax.ShapeDtypeStruct((b, nblk, 8, M_XCHUNK), F32),
        ],
        compiler_params=_params("arbitrary"),
        name=name,
    )(g)


def _lane_pick(x, idx):
    lane = lax.broadcasted_iota(jnp.int32, x.shape, 1)
    return jnp.sum(jnp.where(lane == idx, x, 0.0), axis=1, keepdims=True)


def _chunk_start(c):
    return c * M_XCHUNK if isinstance(c, int) else pl.multiple_of(c * M_XCHUNK, M_XCHUNK)


def _mlstm_kernel(q_ref, k_ref, v_ref, o_ref, gr_ref, gc_ref, gh_ref, out_ref,
                  hf_ref, hb_ref, cn_ref, m_ref, *, seq):
    head = pl.program_id(1)
    nx = seq // M_XCHUNK
    qk_scale = M_DK ** -0.5
    cn_ref[...] = jnp.zeros_like(cn_ref)
    m_ref[...] = jnp.zeros_like(m_ref)

    def step(d, c, size):
        def wide(x, n):
            return x[:, :n] if n < LANES else jnp.concatenate([x] * (n // LANES), axis=1)

        rows = pl.ds(_chunk_start(c), size)
        ti = lax.broadcasted_iota(jnp.int32, (size, size), 0)
        si = lax.broadcasted_iota(jnp.int32, (size, size), 1)
        q = q_ref[rows, :]
        k = k_ref[rows, :]
        v1 = jnp.concatenate([v_ref[rows, :], jnp.ones((size, LANES), BF16)], axis=1)
        beta = jnp.broadcast_to(_lane_pick(gr_ref[0, rows, :], 8 * d + 4 + head), (size, LANES))
        alpha_c = gc_ref[0, c, pl.ds(4 * d + head, 1), :][:, :size]
        m_prev = m_ref[d]
        cn = cn_ref[d]
        keep = (si <= ti) if d == 0 else (si >= ti)
        dmat = jnp.where(keep, wide(beta, size) + alpha_c, -jnp.inf)
        m_inter = beta + m_prev
        m_comb = jnp.maximum(jnp.broadcast_to(jnp.max(dmat, axis=1, keepdims=True), (size, LANES)),
                             m_inter)
        s = lax.dot_general(q, k, _NT, preferred_element_type=F32)
        w = jnp.exp(dmat - wide(m_comb, size)) * s
        scale = jnp.exp(m_inter - m_comb)
        both = (jnp.dot(w.astype(BF16), v1, preferred_element_type=F32)
                + wide(scale, M_DV + LANES)
                * jnp.dot(q, cn.astype(BF16), preferred_element_type=F32)) * qk_scale
        inv = 1.0 / jnp.maximum(jnp.abs(both[:, M_DV:]), jnp.exp(-m_comb))
        hval = both[:, :M_DV] * wide(inv, M_DV)
        if d == 0:
            hf_ref[rows, :] = hval
            tot = beta[size - 1:size]
        else:
            hb_ref[rows, :] = hval
            tot = beta[0:1]
        a_max = jnp.max(alpha_c, axis=1, keepdims=True)
        m_new = jnp.maximum(tot + m_prev, tot + a_max)
        decay = jnp.exp(tot + m_prev - m_new)
        wk = jnp.exp(alpha_c + wide(tot - m_new, size))
        kw = (k.T.astype(F32) * wk).astype(BF16)
        cn_ref[d] = wide(decay, M_DV + LANES) * cn + jnp.dot(kw, v1, preferred_element_type=F32)
        m_ref[d] = m_new

    step(0, nx, TAIL)
    step(1, nx - 1, M_XCHUNK)

    def body(i, carry):
        step(0, i - 1, M_XCHUNK)
        step(1, nx - 1 - i, M_XCHUNK)
        return carry

    lax.fori_loop(1, nx, body, 0, unroll=True)
    step(0, nx - 1, M_XCHUNK)
    step(1, nx, TAIL)

    def finish(c, size):
        rows = pl.ds(_chunk_start(c), size)
        hsum = hf_ref[rows, :] + hb_ref[rows, :]
        og = o_ref[rows, :].astype(F32)
        out_ref[rows, :] = (_rms_rows(hsum, gh_ref[...]) * (1.0 / (1.0 + jnp.exp(-og)))).astype(BF16)

    def finish_body(c, carry):
        finish(c, M_XCHUNK)
        return carry

    lax.fori_loop(0, nx, finish_body, 0)
    finish(nx, TAIL)


def _mlstm(q, k, v, o, grow, gcol, g_head, seq, name):
    _, b, lp, _ = q.shape
    nblk = seq // M_XCHUNK + 1
    per_head = lambda width: pl.BlockSpec((None, None, lp, width), lambda i, h: (h, i, 0, 0))
    return pl.pallas_call(
        functools.partial(_mlstm_kernel, seq=seq),
        grid=(b, M_HEADS),
        in_specs=[
            per_head(M_DK),
            per_head(M_DK),
            per_head(M_DV),
            per_head(M_DV),
            pl.BlockSpec((1, lp, LANES), lambda i, h: (i, 0, 0)),
            pl.BlockSpec((1, nblk, 8, M_XCHUNK), lambda i, h: (i, 0, 0, 0)),
            pl.BlockSpec((1, M_DV), lambda i, h: (0, 0)),
        ],
        out_specs=per_head(M_DV),
        out_shape=jax.ShapeDtypeStruct((M_HEADS, b, lp, M_DV), BF16),
        scratch_shapes=[
            pltpu.VMEM((lp, M_DV), F32),
            pltpu.VMEM((lp, M_DV), F32),
            pltpu.VMEM((2, M_DK, M_DV + LANES), F32),
            pltpu.VMEM((2, 1, LANES), F32),
        ],
        compiler_params=_params("arbitrary", "arbitrary"),
        name=name,
    )(q, k, v, o, grow, gcol, g_head)


ATTN_TK_ONE_CHUNK = 2048
ATTN_TK = 1024
ATTN_SCORE_BYTES = 9 * 1024 * 1024


def _attn_tiles(seq, lp):
    tk = seq if seq <= ATTN_TK_ONE_CHUNK else ATTN_TK
    return _row_tile(lp, ATTN_SCORE_BYTES // (4 * tk)), tk


ATTN_MIN_LOG2_SUM = -60.0
ATTN_BOUND_SLACK = 1.02


def _attn_kernel(q_ref, k_ref, v_ref, lam_ref, gk_ref, gs_ref, out_ref, *, seq, tq, tk, lambda_init):
    lp = seq + TAIL
    q = q_ref[0]
    lane = lax.broadcasted_iota(jnp.int32, (tq, LANES), 1)
    zero = jnp.zeros_like(q)
    qs = (jnp.where(lane < D_HD, q, zero), jnp.where(lane >= D_HD, q, zero))
    kt = k_ref[0, seq:lp, :]
    vt = v_ref[0, seq:lp, :]
    tail_ok = lax.broadcasted_iota(jnp.int32, (tq, TAIL), 1) >= M_PAD
    lam_p = lam_ref[...]
    lam = (jnp.exp(jnp.sum(lam_p[0:1] * lam_p[1:2], axis=1, keepdims=True))
           - jnp.exp(jnp.sum(lam_p[2:3] * lam_p[3:4], axis=1, keepdims=True)) + lambda_init)

    def emit(acc1, l1, acc2, l2):
        o = acc1 / l1 - lam * (acc2 / l2)
        out_ref[0] = (_rms_rows(o, gs_ref[...]) * (1.0 - lambda_init)).astype(BF16)

    group = _group_ones(LANES, D_HD)
    ksq = D_HD * jnp.max(gk_ref[...] * gk_ref[...], axis=1, keepdims=True)
    qf = q.astype(F32)
    qsq = jnp.dot((qf * qf).astype(BF16), group, preferred_element_type=F32)
    bound = jnp.sqrt(qsq * ksq) * ATTN_BOUND_SLACK
    shifts = (jnp.max(jnp.where(lane < D_HD, bound, 0.0), axis=1, keepdims=True),
              jnp.max(jnp.where(lane >= D_HD, bound, 0.0), axis=1, keepdims=True))

    fast = []
    for qc, shift in zip(qs, shifts):
        s = jnp.where(tail_ok, lax.dot_general(qc, kt, _NT, preferred_element_type=F32), NEG)
        e = jnp.exp2(s - shift)
        fast += [jnp.sum(e, axis=1, keepdims=True), jnp.dot(e.astype(BF16), vt, preferred_element_type=F32)]

    def fast_body(j, st):
        rows = pl.ds(pl.multiple_of(j * tk, tk), tk)
        kc = k_ref[0, rows, :]
        vc = v_ref[0, rows, :]
        new = []
        for c, (qc, shift) in enumerate(zip(qs, shifts)):
            l, acc = st[2 * c:2 * c + 2]
            e = jnp.exp2(lax.dot_general(qc, kc, _NT, preferred_element_type=F32) - shift)
            new += [l + jnp.sum(e, axis=1, keepdims=True),
                    acc + jnp.dot(e.astype(BF16), vc, preferred_element_type=F32)]
        return tuple(new)

    l1, acc1, l2, acc2 = lax.fori_loop(0, seq // tk, fast_body, tuple(fast), unroll=True)
    emit(acc1, l1, acc2, l2)

    @pl.when(jnp.logical_not(jnp.min(jnp.minimum(l1, l2)) >= 2.0 ** ATTN_MIN_LOG2_SUM))
    def _():
        state = []
        for qc in qs:
            s = jnp.where(tail_ok, lax.dot_general(qc, kt, _NT, preferred_element_type=F32), NEG)
            m = jnp.max(s, axis=1, keepdims=True)
            e = jnp.exp2(s - m)
            state += [m, jnp.sum(e, axis=1, keepdims=True),
                      jnp.dot(e.astype(BF16), vt, preferred_element_type=F32)]

        def body(j, st):
            rows = pl.ds(pl.multiple_of(j * tk, tk), tk)
            kc = k_ref[0, rows, :]
            vc = v_ref[0, rows, :]
            new = []
            for c, qc in enumerate(qs):
                m, l, acc = st[3 * c:3 * c + 3]
                s = lax.dot_general(qc, kc, _NT, preferred_element_type=F32)
                m_new = jnp.maximum(m, jnp.max(s, axis=1, keepdims=True))
                a = jnp.exp2(m - m_new)
                e = jnp.exp2(s - m_new)
                new += [m_new, a * l + jnp.sum(e, axis=1, keepdims=True),
                        a * acc + jnp.dot(e.astype(BF16), vc, preferred_element_type=F32)]
            return tuple(new)

        _, x1, a1, _, x2, a2 = lax.fori_loop(0, seq // tk, body, tuple(state))
        emit(a1, x1, a2, x2)


def _attn(proj, lam, gk, gs, seq, lambda_init, name):
    b, lp, _ = proj.shape
    tq, tk = _attn_tiles(seq, lp)
    full = lambda i, h, j: (0, 0)
    return pl.pallas_call(
        functools.partial(_attn_kernel, seq=seq, tq=tq, tk=tk, lambda_init=lambda_init),
        grid=(b, D_HEADS, lp // tq),
        in_specs=[
            pl.BlockSpec((1, tq, LANES), lambda i, h, j: (i, j, h)),
            pl.BlockSpec((1, lp, LANES), lambda i, h, j: (i, 0, D_HEADS + h)),
            pl.BlockSpec((1, lp, LANES), lambda i, h, j: (i, 0, 2 * D_HEADS + h)),
            pl.BlockSpec((4, D_HD), full),
            pl.BlockSpec((1, D_HD), full),
            pl.BlockSpec((1, LANES), full),
        ],
        out_specs=pl.BlockSpec((1, tq, LANES), lambda i, h, j: (i, j, h)),
        out_shape=jax.ShapeDtypeStruct((b, lp, D_V_W), BF16),
        compiler_params=_params("arbitrary", "arbitrary", "arbitrary"),
        name=name,
    )(proj, proj, proj, lam, gk, gs)


def _rope_tables(seq, g_q, g_k):
    lp = seq + TAIL
    r = jnp.arange(lp)
    pos = jnp.where(r < seq, r + N_META, jnp.maximum(r - seq - M_PAD, 0)).astype(F32)
    half = ROPE_DIM // 2
    inv = ROPE_THETA ** (-jnp.arange(0, ROPE_DIM, 2, dtype=F32) / ROPE_DIM)
    ang = pos[:, None] * inv[None, :]
    cos, sin = jnp.cos(ang), jnp.sin(ang)
    rest = D_HD - ROPE_DIM

    def fold(g, scale):
        g = g.astype(F32) * scale
        cos_t = jnp.concatenate([cos * g[:half], cos * g[half:ROPE_DIM],
                                 jnp.broadcast_to(g[ROPE_DIM:], (lp, rest))], axis=1)
        sin_dn = jnp.concatenate([-sin * g[half:ROPE_DIM], jnp.zeros((lp, half + rest), F32)], axis=1)
        sin_up = jnp.concatenate([jnp.zeros((lp, half), F32), sin * g[:half],
                                  jnp.zeros((lp, rest), F32)], axis=1)
        return jnp.stack([jnp.tile(t, (1, 2)) for t in (cos_t, sin_dn, sin_up)])

    return jnp.stack([fold(g_q, D_HD ** -0.5 * math.log2(math.e)), fold(g_k, 1.0)])


def _trunk(x, meta_tokens, norm_mix, norm_ffn, m_w_main, m_w_gate, m_b_gate, m_g_head, m_w_out,
           d_w_in, d_g_q, d_g_k, d_lam, d_g_sub, d_w_out, ffn_w_in, ffn_w_out, tag):
    b, seq, d = x.shape
    lp = seq + TAIL
    rows = b * lp
    meta = jnp.broadcast_to(meta_tokens[None].astype(x.dtype), (b, N_META, d))
    h = jnp.concatenate([x, jnp.zeros((b, M_PAD, d), x.dtype), meta], axis=1)
    for i in range(DEPTH):
        j = i // 2
        gmix = norm_mix[i][None]
        hrows = h.reshape(rows, d)
        if i % 2 == 0:
            *qkvo, gates = _m_in(hrows, gmix, m_w_main[j], m_w_gate[j], m_b_gate[j], f"m_in_{tag}{i}")
            grow, gcol = _gate_prep(gates.reshape(b, lp, LANES), seq, f"m_prep_{tag}{i}")
            qkvo = [t.reshape(M_HEADS, b, lp, t.shape[-1]) for t in qkvo]
            mixed = _mlstm(*qkvo, grow, gcol, m_g_head[j][None], seq, f"m_cell_{tag}{i}")
            w_mix_out = m_w_out[j]
        else:
            lambda_init = 0.8 - 0.6 * math.exp(-0.3 * i)
            tables = _rope_tables(seq, d_g_q[j], d_g_k[j])
            proj = _qkv_rope(hrows, gmix, d_w_in[j], tables, lp, f"d_in_{tag}{i}")
            mixed = _attn(proj.reshape(b, lp, 3 * D_QK_W), d_lam[j], d_g_k[j].astype(F32)[None],
                          d_g_sub[j], seq, lambda_init, f"d_attn_{tag}{i}")
            w_mix_out = d_w_out[j]
        h = _ffn(mixed, w_mix_out, h, norm_ffn[i][None], ffn_w_in[i], ffn_w_out[i], f"ffn_{tag}{i}",
                 out_seq=seq if i == DEPTH - 1 else None)
    return h


def kernel(x_prompt, x_sample, meta_tokens, norm_mix, norm_ffn, m_w_in, m_b_gate, m_g_head, m_w_out,
           d_w_in, d_g_q, d_g_k, d_lam_q1, d_lam_k1, d_lam_q2, d_lam_k2, d_g_sub, d_w_out,
           ffn_w_in, ffn_w_out):
    n_gate = 4 * M_HEADS
    m_w_main = m_w_in[:, :, :M_MAIN_W].astype(BF16)
    m_w_gate = jnp.pad(m_w_in[:, :, M_MAIN_W:], ((0, 0), (0, 0), (0, LANES - n_gate))).astype(BF16)
    b_gate = jnp.pad(m_b_gate.astype(F32), ((0, 0), (0, LANES - n_gate)))[:, None, :]
    d_lam = jnp.stack([d_lam_q1, d_lam_k1, d_lam_q2, d_lam_k2], axis=1).astype(F32)
    args = (meta_tokens, norm_mix.astype(F32), norm_ffn.astype(F32), m_w_main, m_w_gate, b_gate,
            m_g_head.astype(F32), m_w_out.astype(BF16), d_w_in.astype(BF16), d_g_q, d_g_k,
            d_lam, d_g_sub.astype(F32)[:, None, :], d_w_out.astype(BF16), ffn_w_in.astype(BF16),
            ffn_w_out.astype(BF16))
    return (_trunk(x_prompt, *args, "p"), _trunk(x_sample, *args, "s"))
```

```python
import functools
import math

import jax
import jax.numpy as jnp
from jax import lax
from jax.experimental import pallas as pl
from jax.experimental.pallas import tpu as pltpu

F32 = jnp.float32
BF16 = jnp.bfloat16

D_MODEL = 1024
DEPTH = 4
N_META = 16
RMS_EPS = 1e-6
NEG = -1e30
M_HEADS = 4
M_DK = 128
M_DV = 256
M_CHUNK = 64
M_XCHUNK = 256
M_PAD = M_CHUNK - N_META
M_QK_W = M_HEADS * M_DK
M_V_W = M_HEADS * M_DV
M_MAIN_W = 2 * M_QK_W + 2 * M_V_W
D_HEADS = 8
D_HD = 64
D_QK_W = D_HEADS * 2 * D_HD
D_V_W = D_HEADS * 2 * D_HD
ROPE_THETA = 500000.0
ROPE_DIM = D_HD // 4
FF = 2816
TAIL = M_CHUNK
LANES = 128
VMEM_LIMIT = 56 * 1024 * 1024


def _params(*sem):
    return pltpu.CompilerParams(dimension_semantics=sem, vmem_limit_bytes=VMEM_LIMIT)


def _row_tile(rows, target):
    best = 16
    for t in range(16, target + 1, 16):
        if rows % t == 0:
            best = t
    return best


def _rms_rows(x, g):
    ms = jnp.mean(x * x, axis=-1, keepdims=True)
    return x * lax.rsqrt(ms + RMS_EPS) * g


MM_COLS = 1024


def _m_in_kernel(x_ref, g_ref, w_ref, wg_ref, bg_ref, q_ref, k_ref, v_ref, o_ref, og_ref):
    xn = _rms_rows(x_ref[...], g_ref[...]).astype(BF16)
    outs = ((q_ref, M_DK), (k_ref, M_DK), (v_ref, M_DV), (o_ref, M_DV))
    col = 0
    for ref, width in outs:
        total = M_HEADS * width
        cols = min(MM_COLS, total)
        for c0 in range(0, total, cols):
            acc = jnp.dot(xn, w_ref[:, col + c0:col + c0 + cols],
                          preferred_element_type=F32).astype(BF16)
            for t in range(cols // width):
                ref[c0 // width + t] = acc[:, t * width:(t + 1) * width]
        col += total
    og_ref[...] = jnp.dot(xn, wg_ref[...], preferred_element_type=F32) + bg_ref[...]


def _m_in(x, g, w, wg, bg, name):
    rows, d = x.shape
    tm = _row_tile(rows, 1056)
    const = lambda i: (0, 0)
    head_major = lambda width: pl.BlockSpec((M_HEADS, tm, width), lambda i: (0, i, 0))
    widths = (M_DK, M_DK, M_DV, M_DV)
    return pl.pallas_call(
        _m_in_kernel,
        grid=(rows // tm,),
        in_specs=[
            pl.BlockSpec((tm, d), lambda i: (i, 0)),
            pl.BlockSpec((1, d), const),
            pl.BlockSpec((d, M_MAIN_W), const, pipeline_mode=pl.Buffered(1)),
            pl.BlockSpec((d, LANES), const, pipeline_mode=pl.Buffered(1)),
            pl.BlockSpec((1, LANES), const),
        ],
        out_specs=[head_major(wd) for wd in widths] + [pl.BlockSpec((tm, LANES), lambda i: (i, 0))],
        out_shape=[jax.ShapeDtypeStruct((M_HEADS, rows, wd), BF16) for wd in widths]
        + [jax.ShapeDtypeStruct((rows, LANES), F32)],
        compiler_params=_params("arbitrary"),
        name=name,
    )(x, g, w, wg, bg)


def _group_ones(width, group):
    gi = lax.broadcasted_iota(jnp.int32, (width, width), 0) // group
    gj = lax.broadcasted_iota(jnp.int32, (width, width), 1) // group
    return jnp.where(gi == gj, 1.0, 0.0).astype(BF16)


def _qkv_rope_kernel(x_ref, g_ref, w_ref, tab_ref, o_ref):
    xn = _rms_rows(x_ref[...], g_ref[...]).astype(BF16)
    wide = 2 * LANES
    group = _group_ones(wide, D_HD)
    half = ROPE_DIM // 2
    for j in range(3):
        if j == 2:
            o_ref[:, j * D_QK_W:(j + 1) * D_QK_W] = jnp.dot(
                xn, w_ref[:, j * D_QK_W:(j + 1) * D_QK_W], preferred_element_type=F32).astype(BF16)
            continue
        for p in range(D_QK_W // wide):
            x2 = jnp.dot(xn, w_ref[:, j * D_QK_W + p * wide:j * D_QK_W + (p + 1) * wide],
                         preferred_element_type=F32)
            ssq = jnp.dot((x2 * x2).astype(BF16), group, preferred_element_type=F32)
            xs = x2 * lax.rsqrt(ssq * (1.0 / D_HD) + RMS_EPS)
            for hf in range(2):
                xh = xs[:, hf * LANES:(hf + 1) * LANES]
                out = (xh * tab_ref[j, 0] + pltpu.roll(xh, LANES - half, 1) * tab_ref[j, 1]
                       + pltpu.roll(xh, half, 1) * tab_ref[j, 2])
                c0 = j * D_QK_W + p * wide + hf * LANES
                o_ref[:, c0:c0 + LANES] = out.astype(BF16)


def _qkv_rope(x, g, w, tables, lp, name):
    rows, d = x.shape
    n = w.shape[1]
    tm = _row_tile(lp, 1056)
    nb = lp // tm
    return pl.pallas_call(
        _qkv_rope_kernel,
        grid=(rows // tm,),
        in_specs=[
            pl.BlockSpec((tm, d), lambda i: (i, 0)),
            pl.BlockSpec((1, d), lambda i: (0, 0)),
            pl.BlockSpec((d, n), lambda i: (0, 0), pipeline_mode=pl.Buffered(1)),
            pl.BlockSpec((2, 3, tm, LANES), lambda i: (0, 0, i % nb, 0)),
        ],
        out_specs=pl.BlockSpec((tm, n), lambda i: (i, 0)),
        out_shape=jax.ShapeDtypeStruct((rows, n), BF16),
        compiler_params=_params("arbitrary"),
        name=name,
    )(x, g, w, tables)


FF_CHUNK = 512
FFN_ROWS = 512


def _ffn_kernel(a_ref, wm_ref, h_ref, g_ref, wi_ref, wo_ref, o_ref, xn_ref):
    if len(a_ref.shape) == 3:
        a = jnp.concatenate([a_ref[hd] for hd in range(a_ref.shape[0])], axis=1)
    else:
        a = a_ref[...]
    x = h_ref[...] + jnp.dot(a, wm_ref[...], preferred_element_type=F32)
    xn_ref[...] = _rms_rows(x, g_ref[...]).astype(BF16)
    o_ref[...] = x
    for c0 in range(0, FF, FF_CHUNK):
        c1 = min(c0 + FF_CHUNK, FF)
        xn = xn_ref[...]
        gate = jnp.dot(xn, wi_ref[:, c0:c1], preferred_element_type=F32)
        up = jnp.dot(xn, wi_ref[:, FF + c0:FF + c1], preferred_element_type=F32)
        act = (gate * (1.0 / (1.0 + jnp.exp(-gate))) * up).astype(BF16)
        o_ref[...] += jnp.dot(act, wo_ref[c0:c1, :], preferred_element_type=F32)


def _ffn(a, wm, h, g, wi, wo, name, out_seq=None):
    b, lp, d = h.shape
    const = lambda *_: (0, 0)
    weights = [
        pl.BlockSpec((d, d), const, pipeline_mode=pl.Buffered(1)),
        pl.BlockSpec((1, d), const),
        pl.BlockSpec((d, 2 * FF), const, pipeline_mode=pl.Buffered(1)),
        pl.BlockSpec((FF, d), const, pipeline_mode=pl.Buffered(1)),
    ]
    if out_seq is None:
        rows = b * lp
        tm = _row_tile(rows, FFN_ROWS)
        grid = (rows // tm,)
        tile = pl.BlockSpec((tm, d), lambda i: (i, 0))
        h = h.reshape(rows, d)
        if a.ndim == 4:
            a = a.reshape(a.shape[0], rows, a.shape[3])
            a_tile = pl.BlockSpec((a.shape[0], tm, a.shape[2]), lambda i: (0, i, 0))
        else:
            a, a_tile = a.reshape(rows, d), tile
        out_shape = jax.ShapeDtypeStruct((rows, d), F32)
        sem = ("arbitrary",)
    else:
        tm = _row_tile(out_seq, FFN_ROWS)
        grid = (b, out_seq // tm)
        tile = a_tile = pl.BlockSpec((None, tm, d), lambda i, j: (i, j, 0))
        if a.ndim == 4:
            a_tile = pl.BlockSpec((a.shape[0], None, tm, a.shape[3]), lambda i, j: (0, i, j, 0))
        out_shape = jax.ShapeDtypeStruct((b, out_seq, d), F32)
        sem = ("arbitrary", "arbitrary")
    out = pl.pallas_call(
        _ffn_kernel,
        grid=grid,
        in_specs=[a_tile, weights[0], tile, weights[1], weights[2], weights[3]],
        out_specs=tile,
        out_shape=out_shape,
        scratch_shapes=[pltpu.VMEM((tm, d), BF16)],
        compiler_params=_params(*sem),
        name=name,
    )(a, wm, h, g, wi, wo)
    return out if out_seq is not None else out.reshape(b, lp, d)


def _split3(x):
    hi = x.astype(BF16)
    r1 = x - hi.astype(F32)
    mid = r1.astype(BF16)
    lo = (r1 - mid.astype(F32)).astype(BF16)
    return hi, mid, lo


def _dot_exact01(m01, x, dims):
    out = None
    for part in _split3(x):
        t = lax.dot_general(m01, part, dims, preferred_element_type=F32)
        out = t if out is None else out + t
    return out


_NN = (((1,), (0,)), ((), ()))
_NT = (((1,), (1,)), ((), ()))


def _gate_prep_kernel(g_ref, row_ref, col_ref, *, seq):
    nx = seq // M_XCHUNK
    er = lax.broadcasted_iota(jnp.int32, (8, LANES), 0)
    el = lax.broadcasted_iota(jnp.int32, (8, LANES), 1)
    pick = jnp.where(el == jnp.where(er < 4, er, er + 4), 1.0, 0.0).astype(BF16)
    for j in range(nx + 1):
        size = M_XCHUNK if j < nx else TAIL
        rows = slice(j * M_XCHUNK, j * M_XCHUNK + size)
        g = g_ref[0, rows, :]
        logf = jnp.minimum(g, 0.0) - jnp.log(1.0 + jnp.exp(-jnp.abs(g)))
        ig = g
        if j == nx:
            real = lax.broadcasted_iota(jnp.int32, (size, LANES), 0) >= M_PAD
            logf = jnp.where(real, logf, 0.0)
            ig = jnp.where(real, g, NEG)
        ti = lax.broadcasted_iota(jnp.int32, (size, size), 0)
        si = lax.broadcasted_iota(jnp.int32, (size, size), 1)
        pre_m = jnp.where(si <= ti, 1.0, 0.0).astype(BF16)
        pre = _dot_exact01(pre_m, logf, _NN)
        suf = pre[size - 1:size, :] - pre + logf
        lane = lax.broadcasted_iota(jnp.int32, (size, LANES), 1)
        pre_dn = pltpu.roll(pre, LANES - M_HEADS, 1)
        suf_dn = pltpu.roll(suf, LANES - M_HEADS, 1)
        row = jnp.where(lane < 4, ig - pre_dn,
                        jnp.where(lane < 8, pre,
                                  jnp.where(lane < 12, ig - suf_dn, suf)))
        row = jnp.where(lane < 16, row, 0.0)
        row_ref[0, rows, :] = row
        col = _dot_exact01(pick, row, _NT)
        if j == nx:
            col_ref[0, j] = jnp.zeros((8, M_XCHUNK), F32)
            col_ref[0, j, :, 0:size] = col
        else:
            col_ref[0, j] = col


def _gate_prep(g, seq, name):
    b, lp, _ = g.shape
    nblk = seq // M_XCHUNK + 1
    return pl.pallas_call(
        functools.partial(_gate_prep_kernel, seq=seq),
        grid=(b,),
        in_specs=[pl.BlockSpec((1, lp, LANES), lambda i: (i, 0, 0))],
        out_specs=[
            pl.BlockSpec((1, lp, LANES), lambda i: (i, 0, 0)),
            pl.BlockSpec((1, nblk, 8, M_XCHUNK), lambda i: (i, 0, 0, 0)),
        ],
        out_shape=[
            jax.ShapeDtypeStruct((b, lp, LANES), F32),
            jax.ShapeDtypeStruct((b, nblk, 8, M_XCHUNK), F32),
        ],
        compiler_params=_params("arbitrary"),
        name=name,
    )(g)


def _lane_pick(x, idx):
    lane = lax.broadcasted_iota(jnp.int32, x.shape, 1)
    return jnp.sum(jnp.where(lane == idx, x, 0.0), axis=1, keepdims=True)


def _chunk_start(c):
    return c * M_XCHUNK if isinstance(c, int) else pl.multiple_of(c * M_XCHUNK, M_XCHUNK)


def _mlstm_kernel(q_ref, k_ref, v_ref, o_ref, gr_ref, gc_ref, gh_ref, out_ref,
                  hf_ref, hb_ref, cn_ref, m_ref, *, seq):
    head = pl.program_id(1)
    nx = seq // M_XCHUNK
    qk_scale = M_DK ** -0.5
    cn_ref[...] = jnp.zeros_like(cn_ref)
    m_ref[...] = jnp.zeros_like(m_ref)

    def step(d, c, size):
        def wide(x, n):
            return x[:, :n] if n < LANES else jnp.concatenate([x] * (n // LANES), axis=1)

        rows = pl.ds(_chunk_start(c), size)
        ti = lax.broadcasted_iota(jnp.int32, (size, size), 0)
        si = lax.broadcasted_iota(jnp.int32, (size, size), 1)
        q = q_ref[rows, :]
        k = k_ref[rows, :]
        v1 = jnp.concatenate([v_ref[rows, :], jnp.ones((size, LANES), BF16)], axis=1)
        beta = jnp.broadcast_to(_lane_pick(gr_ref[0, rows, :], 8 * d + 4 + head), (size, LANES))
        alpha_c = gc_ref[0, c, pl.ds(4 * d + head, 1), :][:, :size]
        m_prev = m_ref[d]
        cn = cn_ref[d]
        keep = (si <= ti) if d == 0 else (si >= ti)
        dmat = jnp.where(keep, wide(beta, size) + alpha_c, -jnp.inf)
        m_inter = beta + m_prev
        m_comb = jnp.maximum(jnp.broadcast_to(jnp.max(dmat, axis=1, keepdims=True), (size, LANES)),
                             m_inter)
        s = lax.dot_general(q, k, _NT, preferred_element_type=F32)
        w = jnp.exp(dmat - wide(m_comb, size)) * s
        scale = jnp.exp(m_inter - m_comb)
        both = (jnp.dot(w.astype(BF16), v1, preferred_element_type=F32)
                + wide(scale, M_DV + LANES)
                * jnp.dot(q, cn.astype(BF16), preferred_element_type=F32)) * qk_scale
        inv = 1.0 / jnp.maximum(jnp.abs(both[:, M_DV:]), jnp.exp(-m_comb))
        hval = both[:, :M_DV] * wide(inv, M_DV)
        if d == 0:
            hf_ref[rows, :] = hval
            tot = beta[size - 1:size]
        else:
            hb_ref[rows, :] = hval
            tot = beta[0:1]
        a_max = jnp.max(alpha_c, axis=1, keepdims=True)
        m_new = jnp.maximum(tot + m_prev, tot + a_max)
        decay = jnp.exp(tot + m_prev - m_new)
        wk = jnp.exp(alpha_c + wide(tot - m_new, size))
        kw = (k.T.astype(F32) * wk).astype(BF16)
        cn_ref[d] = wide(decay, M_DV + LANES) * cn + jnp.dot(kw, v1, preferred_element_type=F32)
        m_ref[d] = m_new

    step(0, nx, TAIL)
    step(1, nx - 1, M_XCHUNK)

    def body(i, carry):
        step(0, i - 1, M_XCHUNK)
        step(1, nx - 1 - i, M_XCHUNK)
        return carry

    lax.fori_loop(1, nx, body, 0, unroll=True)
    step(0, nx - 1, M_XCHUNK)
    step(1, nx, TAIL)

    def finish(c, size):
        rows = pl.ds(_chunk_start(c), size)
        hsum = hf_ref[rows, :] + hb_ref[rows, :]
        og = o_ref[rows, :].astype(F32)
        out_ref[rows, :] = (_rms_rows(hsum, gh_ref[...]) * (1.0 / (1.0 + jnp.exp(-og)))).astype(BF16)

    def finish_body(c, carry):
        finish(c, M_XCHUNK)
        return carry

    lax.fori_loop(0, nx, finish_body, 0)
    finish(nx, TAIL)


def _mlstm(q, k, v, o, grow, gcol, g_head, seq, name):
    _, b, lp, _ = q.shape
    nblk = seq // M_XCHUNK + 1
    per_head = lambda width: pl.BlockSpec((None, None, lp, width), lambda i, h: (h, i, 0, 0))
    return pl.pallas_call(
        functools.partial(_mlstm_kernel, seq=seq),
        grid=(b, M_HEADS),
        in_specs=[
            per_head(M_DK),
            per_head(M_DK),
            per_head(M_DV),
            per_head(M_DV),
            pl.BlockSpec((1, lp, LANES), lambda i, h: (i, 0, 0)),
            pl.BlockSpec((1, nblk, 8, M_XCHUNK), lambda i, h: (i, 0, 0, 0)),
            pl.BlockSpec((1, M_DV), lambda i, h: (0, 0)),
        ],
        out_specs=per_head(M_DV),
        out_shape=jax.ShapeDtypeStruct((M_HEADS, b, lp, M_DV), BF16),
        scratch_shapes=[
            pltpu.VMEM((lp, M_DV), F32),
            pltpu.VMEM((lp, M_DV), F32),
            pltpu.VMEM((2, M_DK, M_DV + LANES), F32),
            pltpu.VMEM((2, 1, LANES), F32),
        ],
        compiler_params=_params("arbitrary", "arbitrary"),
        name=name,
    )(q, k, v, o, grow, gcol, g_head)


ATTN_TK_ONE_CHUNK = 2048
ATTN_TK = 1024
ATTN_SCORE_BYTES = 9 * 1024 * 1024


def _attn_tiles(seq, lp):
    tk = seq if seq <= ATTN_TK_ONE_CHUNK else ATTN_TK
    return _row_tile(lp, ATTN_SCORE_BYTES // (4 * tk)), tk


ATTN_MIN_LOG2_SUM = -60.0
ATTN_BOUND_SLACK = 1.02


def _attn_kernel(q_ref, k_ref, v_ref, lam_ref, gk_ref, gs_ref, out_ref, *, seq, tq, tk, lambda_init):
    lp = seq + TAIL
    q = q_ref[0]
    lane = lax.broadcasted_iota(jnp.int32, (tq, LANES), 1)
    zero = jnp.zeros_like(q)
    qs = (jnp.where(lane < D_HD, q, zero), jnp.where(lane >= D_HD, q, zero))
    kt = k_ref[0, seq:lp, :]
    vt = v_ref[0, seq:lp, :]
    tail_ok = lax.broadcasted_iota(jnp.int32, (tq, TAIL), 1) >= M_PAD
    lam_p = lam_ref[...]
    lam = (jnp.exp(jnp.sum(lam_p[0:1] * lam_p[1:2], axis=1, keepdims=True))
           - jnp.exp(jnp.sum(lam_p[2:3] * lam_p[3:4], axis=1, keepdims=True)) + lambda_init)

    def emit(acc1, l1, acc2, l2):
        o = acc1 / l1 - lam * (acc2 / l2)
        out_ref[0] = (_rms_rows(o, gs_ref[...]) * (1.0 - lambda_init)).astype(BF16)

    group = _group_ones(LANES, D_HD)
    ksq = D_HD * jnp.max(gk_ref[...] * gk_ref[...], axis=1, keepdims=True)
    qf = q.astype(F32)
    qsq = jnp.dot((qf * qf).astype(BF16), group, preferred_element_type=F32)
    bound = jnp.sqrt(qsq * ksq) * ATTN_BOUND_SLACK
    shifts = (jnp.max(jnp.where(lane < D_HD, bound, 0.0), axis=1, keepdims=True),
              jnp.max(jnp.where(lane >= D_HD, bound, 0.0), axis=1, keepdims=True))

    fast = []
    for qc, shift in zip(qs, shifts):
        s = jnp.where(tail_ok, lax.dot_general(qc, kt, _NT, preferred_element_type=F32), NEG)
        e = jnp.exp2(s - shift)
        fast += [jnp.sum(e, axis=1, keepdims=True), jnp.dot(e.astype(BF16), vt, preferred_element_type=F32)]

    def fast_body(j, st):
        rows = pl.ds(pl.multiple_of(j * tk, tk), tk)
        kc = k_ref[0, rows, :]
        vc = v_ref[0, rows, :]
        new = []
        for c, (qc, shift) in enumerate(zip(qs, shifts)):
            l, acc = st[2 * c:2 * c + 2]
            e = jnp.exp2(lax.dot_general(qc, kc, _NT, preferred_element_type=F32) - shift)
            new += [l + jnp.sum(e, axis=1, keepdims=True),
                    acc + jnp.dot(e.astype(BF16), vc, preferred_element_type=F32)]
        return tuple(new)

    l1, acc1, l2, acc2 = lax.fori_loop(0, seq // tk, fast_body, tuple(fast), unroll=True)
    emit(acc1, l1, acc2, l2)

    @pl.when(jnp.logical_not(jnp.min(jnp.minimum(l1, l2)) >= 2.0 ** ATTN_MIN_LOG2_SUM))
    def _():
        state = []
        for qc in qs:
            s = jnp.where(tail_ok, lax.dot_general(qc, kt, _NT, preferred_element_type=F32), NEG)
            m = jnp.max(s, axis=1, keepdims=True)
            e = jnp.exp2(s - m)
            state += [m, jnp.sum(e, axis=1, keepdims=True),
                      jnp.dot(e.astype(BF16), vt, preferred_element_type=F32)]

        def body(j, st):
            rows = pl.ds(pl.multiple_of(j * tk, tk), tk)
            kc = k_ref[0, rows, :]
            vc = v_ref[0, rows, :]
            new = []
            for c, qc in enumerate(qs):
                m, l, acc = st[3 * c:3 * c + 3]
                s = lax.dot_general(qc, kc, _NT, preferred_element_type=F32)
                m_new = jnp.maximum(m, jnp.max(s, axis=1, keepdims=True))
                a = jnp.exp2(m - m_new)
                e = jnp.exp2(s - m_new)
                new += [m_new, a * l + jnp.sum(e, axis=1, keepdims=True),
                        a * acc + jnp.dot(e.astype(BF16), vc, preferred_element_type=F32)]
            return tuple(new)

        _, x1, a1, _, x2, a2 = lax.fori_loop(0, seq // tk, body, tuple(state))
        emit(a1, x1, a2, x2)


def _attn(proj, lam, gk, gs, seq, lambda_init, name):
    b, lp, _ = proj.shape
    tq, tk = _attn_tiles(seq, lp)
    full = lambda i, h, j: (0, 0)
    return pl.pallas_call(
        functools.partial(_attn_kernel, seq=seq, tq=tq, tk=tk, lambda_init=lambda_init),
        grid=(b, D_HEADS, lp // tq),
        in_specs=[
            pl.BlockSpec((1, tq, LANES), lambda i, h, j: (i, j, h)),
            pl.BlockSpec((1, lp, LANES), lambda i, h, j: (i, 0, D_HEADS + h)),
            pl.BlockSpec((1, lp, LANES), lambda i, h, j: (i, 0, 2 * D_HEADS + h)),
            pl.BlockSpec((4, D_HD), full),
            pl.BlockSpec((1, D_HD), full),
            pl.BlockSpec((1, LANES), full),
        ],
        out_specs=pl.BlockSpec((1, tq, LANES), lambda i, h, j: (i, j, h)),
        out_shape=jax.ShapeDtypeStruct((b, lp, D_V_W), BF16),
        compiler_params=_params("arbitrary", "arbitrary", "arbitrary"),
        name=name,
    )(proj, proj, proj, lam, gk, gs)


def _rope_tables(seq, g_q, g_k):
    lp = seq + TAIL
    r = jnp.arange(lp)
    pos = jnp.where(r < seq, r + N_META, jnp.maximum(r - seq - M_PAD, 0)).astype(F32)
    half = ROPE_DIM // 2
    inv = ROPE_THETA ** (-jnp.arange(0, ROPE_DIM, 2, dtype=F32) / ROPE_DIM)
    ang = pos[:, None] * inv[None, :]
    cos, sin = jnp.cos(ang), jnp.sin(ang)
    rest = D_HD - ROPE_DIM

    def fold(g, scale):
        g = g.astype(F32) * scale
        cos_t = jnp.concatenate([cos * g[:half], cos * g[half:ROPE_DIM],
                                 jnp.broadcast_to(g[ROPE_DIM:], (lp, rest))], axis=1)
        sin_dn = jnp.concatenate([-sin * g[half:ROPE_DIM], jnp.zeros((lp, half + rest), F32)], axis=1)
        sin_up = jnp.concatenate([jnp.zeros((lp, half), F32), sin * g[:half],
                                  jnp.zeros((lp, rest), F32)], axis=1)
        return jnp.stack([jnp.tile(t, (1, 2)) for t in (cos_t, sin_dn, sin_up)])

    return jnp.stack([fold(g_q, D_HD ** -0.5 * math.log2(math.e)), fold(g_k, 1.0)])


def _trunk(x, meta_tokens, norm_mix, norm_ffn, m_w_main, m_w_gate, m_b_gate, m_g_head, m_w_out,
           d_w_in, d_g_q, d_g_k, d_lam, d_g_sub, d_w_out, ffn_w_in, ffn_w_out, tag):
    b, seq, d = x.shape
    lp = seq + TAIL
    rows = b * lp
    meta = jnp.broadcast_to(meta_tokens[None].astype(x.dtype), (b, N_META, d))
    h = jnp.concatenate([x, jnp.zeros((b, M_PAD, d), x.dtype), meta], axis=1)
    for i in range(DEPTH):
        j = i // 2
        gmix = norm_mix[i][None]
        hrows = h.reshape(rows, d)
        if i % 2 == 0:
            *qkvo, gates = _m_in(hrows, gmix, m_w_main[j], m_w_gate[j], m_b_gate[j], f"m_in_{tag}{i}")
            grow, gcol = _gate_prep(gates.reshape(b, lp, LANES), seq, f"m_prep_{tag}{i}")
            qkvo = [t.reshape(M_HEADS, b, lp, t.shape[-1]) for t in qkvo]
            mixed = _mlstm(*qkvo, grow, gcol, m_g_head[j][None], seq, f"m_cell_{tag}{i}")
            w_mix_out = m_w_out[j]
        else:
            lambda_init = 0.8 - 0.6 * math.exp(-0.3 * i)
            tables = _rope_tables(seq, d_g_q[j], d_g_k[j])
            proj = _qkv_rope(hrows, gmix, d_w_in[j], tables, lp, f"d_in_{tag}{i}")
            mixed = _attn(proj.reshape(b, lp, 3 * D_QK_W), d_lam[j], d_g_k[j].astype(F32)[None],
                          d_g_sub[j], seq, lambda_init, f"d_attn_{tag}{i}")
            w_mix_out = d_w_out[j]
        h = _ffn(mixed, w_mix_out, h, norm_ffn[i][None], ffn_w_in[i], ffn_w_out[i], f"ffn_{tag}{i}",
                 out_seq=seq if i == DEPTH - 1 else None)
    return h


def kernel(x_prompt, x_sample, meta_tokens, norm_mix, norm_ffn, m_w_in, m_b_gate, m_g_head, m_w_out,
           d_w_in, d_g_q, d_g_k, d_lam_q1, d_lam_k1, d_lam_q2, d_lam_k2, d_g_sub, d_w_out,
           ffn_w_in, ffn_w_out):
    n_gate = 4 * M_HEADS
    m_w_main = m_w_in[:, :, :M_MAIN_W].astype(BF16)
    m_w_gate = jnp.pad(m_w_in[:, :, M_MAIN_W:], ((0, 0), (0, 0), (0, LANES - n_gate))).astype(BF16)
    b_gate = jnp.pad(m_b_gate.astype(F32), ((0, 0), (0, LANES - n_gate)))[:, None, :]
    d_lam = jnp.stack([d_lam_q1, d_lam_k1, d_lam_q2, d_lam_k2], axis=1).astype(F32)
    args = (meta_tokens, norm_mix.astype(F32), norm_ffn.astype(F32), m_w_main, m_w_gate, b_gate,
            m_g_head.astype(F32), m_w_out.astype(BF16), d_w_in.astype(BF16), d_g_q, d_g_k,
            d_lam, d_g_sub.astype(F32)[:, None, :], d_w_out.astype(BF16), ffn_w_in.astype(BF16),
            ffn_w_out.astype(BF16))
    return (_trunk(x_prompt, *args, "p"), _trunk(x_sample, *args, "s"))
```
